```python
import math
import jax, jax.numpy as jnp
from jax import lax
import numpy as np

D_MODEL = 1024
BATCH = 16
SEQ = 4096
DEPTH = 2

N_META = 16
N_A_LAYERS = DEPTH // 2
N_B_LAYERS = DEPTH - N_A_LAYERS
D_FF = 2816
SSM_WIDTH = D_MODEL // 2
SSM_GROUP = 16
SSM_GROUPS = SSM_WIDTH // SSM_GROUP
SSM_STATE = 64
STEP_MIN = 1e-3
STEP_MAX = 1e-1
HEAD_DIM = 64
N_Q_HEADS = D_MODEL // HEAD_DIM
N_KV_HEADS = 4
Q_PER_KV = N_Q_HEADS // N_KV_HEADS
WINDOW = 128
BLOCK = 128
ROPE_THETA = 10000.0
EPS = 1e-6
NEG_INF = -1e30

kernel_name = "yoco_s5_swa_sink_macaron"


def rms_norm(x, g):
    xf = x.astype(jnp.float32)
    y = xf * lax.rsqrt(jnp.mean(xf * xf, axis=-1, keepdims=True) + EPS)
    return (y * g.astype(jnp.float32)).astype(x.dtype)


def rope(x, pos):
    half = HEAD_DIM // 2
    freqs = ROPE_THETA ** (-jnp.arange(0, half, dtype=jnp.float32) * 2.0 / HEAD_DIM)
    ang = pos.astype(jnp.float32)[:, None] * freqs[None, :]
    bshape = (pos.shape[0],) + (1,) * (x.ndim - 3) + (half,)
    cos = jnp.cos(ang).reshape(bshape)
    sin = jnp.sin(ang).reshape(bshape)
    xf = x.astype(jnp.float32)
    x1, x2 = xf[..., :half], xf[..., half:]
    return jnp.concatenate([x1 * cos - x2 * sin, x2 * cos + x1 * sin], axis=-1).astype(x.dtype)


def swiglu_ffn(h, g, w_gate_up, w_down):
    a, b = jnp.split(rms_norm(h, g) @ w_gate_up, 2, axis=-1)
    return (jax.nn.silu(a) * b) @ w_down


def _complex_scan_op(e1, e2):
    a1r, a1i, b1r, b1i = e1
    a2r, a2i, b2r, b2i = e2
    return (a2r * a1r - a2i * a1i,
            a2r * a1i + a2i * a1r,
            a2r * b1r - a2i * b1i + b2r,
            a2r * b1i + a2i * b1r + b2i)


def s5_mixer(hn, w_in, lam_re, lam_im, b_re, b_im, c_re, c_im, log_step, d_skip, w_out):
    bsz, L, _ = hn.shape
    f = lambda t: t.astype(jnp.float32)
    u = f(hn @ w_in)
    ug = u.reshape(bsz, L, SSM_GROUPS, SSM_GROUP)
    lr, li = f(lam_re), f(lam_im)
    step = jnp.exp(f(log_step))[:, None]
    mag = jnp.exp(lr * step)
    ar = mag * jnp.cos(li * step)
    ai = mag * jnp.sin(li * step)
    den = lr * lr + li * li
    nr, ni = ar - 1.0, ai
    cr = (nr * lr + ni * li) / den
    ci = (ni * lr - nr * li) / den
    br, bi = f(b_re), f(b_im)
    bbar_r = cr[..., None] * br - ci[..., None] * bi
    bbar_i = cr[..., None] * bi + ci[..., None] * br
    bu_r = jnp.einsum('blgc,gpc->blgp', ug, bbar_r)
    bu_i = jnp.einsum('blgc,gpc->blgp', ug, bbar_i)
    a_r = jnp.broadcast_to(ar, (1, L, SSM_GROUPS, SSM_STATE))
    a_i = jnp.broadcast_to(ai, (1, L, SSM_GROUPS, SSM_STATE))
    _, _, xr, xi = lax.associative_scan(_complex_scan_op, (a_r, a_i, bu_r, bu_i), axis=1)
    y = jnp.einsum('blgp,gcp->blgc', xr, f(c_re)) - jnp.einsum('blgp,gcp->blgc', xi, f(c_im))
    y = y.reshape(bsz, L, SSM_WIDTH) + f(d_skip) * u
    z = jax.nn.gelu(y).astype(hn.dtype) @ w_out
    a, g = jnp.split(z, 2, axis=-1)
    return a * jax.nn.sigmoid(g)


def shared_kv(h, g_kv, w_kv, k_gain):
    bsz, L, _ = h.shape
    k, v = jnp.split(rms_norm(h, g_kv) @ w_kv, 2, axis=-1)
    k = k.reshape(bsz, L, N_KV_HEADS, HEAD_DIM)
    v = v.reshape(bsz, L, N_KV_HEADS, HEAD_DIM)
    k = rope(rms_norm(k, k_gain), jnp.arange(L))
    return k, v


def swa_sink_attention(hn, k, v, w_q, q_gain, sinks, w_o):
    bsz, S, _ = hn.shape
    nb = S // BLOCK
    q = (hn @ w_q).reshape(bsz, S, N_KV_HEADS, Q_PER_KV, HEAD_DIM)
    q = rope(rms_norm(q, q_gain), N_META + jnp.arange(S))
    qb = q.reshape(bsz, nb, BLOCK, N_KV_HEADS, Q_PER_KV, HEAD_DIM)
    k_meta, v_meta = k[:, :N_META], v[:, :N_META]
    k_blk = k[:, N_META:].reshape(bsz, nb, BLOCK, N_KV_HEADS, HEAD_DIM)
    v_blk = v[:, N_META:].reshape(bsz, nb, BLOCK, N_KV_HEADS, HEAD_DIM)
    pad = ((0, 0), (1, 0), (0, 0), (0, 0), (0, 0))
    k_band = jnp.concatenate([jnp.pad(k_blk, pad)[:, :-1], k_blk], axis=2)
    v_band = jnp.concatenate([jnp.pad(v_blk, pad)[:, :-1], v_blk], axis=2)
    scale = HEAD_DIM ** -0.5
    s_band = jnp.einsum('bnqhgd,bnkhd->bnhgqk', qb, k_band,
                        preferred_element_type=jnp.float32) * scale
    qi = jnp.arange(BLOCK)[:, None]
    kj = jnp.arange(2 * BLOCK)[None, :]
    rel = qi + BLOCK - kj
    blk = jnp.arange(nb)[:, None, None]
    valid = (rel >= 0) & (rel < WINDOW) & ((blk > 0) | (kj >= BLOCK))
    s_band = jnp.where(valid[None, :, None, None], s_band, NEG_INF)
    s_meta = jnp.einsum('bnqhgd,bmhd->bnhgqm', qb, k_meta,
                        preferred_element_type=jnp.float32) * scale
    sink = sinks.astype(jnp.float32).reshape(N_KV_HEADS, Q_PER_KV)[None, None, :, :, None]
    m = jnp.maximum(jnp.maximum(s_band.max(-1), s_meta.max(-1)), sink)
    p_band = jnp.exp(s_band - m[..., None])
    p_meta = jnp.exp(s_meta - m[..., None])
    denom = p_band.sum(-1) + p_meta.sum(-1) + jnp.exp(sink - m)
    o = (jnp.einsum('bnhgqk,bnkhd->bnqhgd', p_band, v_band.astype(jnp.float32))
         + jnp.einsum('bnhgqm,bmhd->bnqhgd', p_meta, v_meta.astype(jnp.float32)))
    o = o / jnp.moveaxis(denom, -1, 2)[..., None]
    return o.reshape(bsz, S, N_Q_HEADS * HEAD_DIM).astype(hn.dtype) @ w_o


def setup_inputs(seed: int = 0) -> dict:
    key = jax.random.key(seed)
    ks = jax.random.split(key, 32)
    f32 = jnp.float32

    def nrm(k, shape, scale):
        return jax.random.normal(k, shape, f32) * scale

    H, G, P, C = SSM_WIDTH, SSM_GROUPS, SSM_STATE, SSM_GROUP
    return {
        "x": nrm(ks[0], (BATCH, SEQ, D_MODEL), 1.0),
        "meta_tokens": nrm(ks[1], (N_META, D_MODEL), 1.0),
        "ffn1_norm": 1.0 + nrm(ks[2], (DEPTH, D_MODEL), 0.02),
        "ffn1_w_gate_up": nrm(ks[3], (DEPTH, D_MODEL, 2 * D_FF), D_MODEL ** -0.5),
        "ffn1_w_down": nrm(ks[4], (DEPTH, D_FF, D_MODEL), D_FF ** -0.5),
        "mix_norm": 1.0 + nrm(ks[5], (DEPTH, D_MODEL), 0.02),
        "ffn2_norm": 1.0 + nrm(ks[6], (DEPTH, D_MODEL), 0.02),
        "ffn2_w_gate_up": nrm(ks[7], (DEPTH, D_MODEL, 2 * D_FF), D_MODEL ** -0.5),
        "ffn2_w_down": nrm(ks[8], (DEPTH, D_FF, D_MODEL), D_FF ** -0.5),
        "ssm_w_in": nrm(ks[9], (N_A_LAYERS, D_MODEL, H), D_MODEL ** -0.5),
        "ssm_lambda_re": -0.5 + nrm(ks[10], (N_A_LAYERS, G, P), 0.01),
        "ssm_lambda_im": jnp.pi * jnp.arange(P, dtype=f32) + nrm(ks[11], (N_A_LAYERS, G, P), 0.01),
        "ssm_b_re": nrm(ks[12], (N_A_LAYERS, G, P, C), (2 * C) ** -0.5),
        "ssm_b_im": nrm(ks[13], (N_A_LAYERS, G, P, C), (2 * C) ** -0.5),
        "ssm_c_re": nrm(ks[14], (N_A_LAYERS, G, C, P), P ** -0.5),
        "ssm_c_im": nrm(ks[15], (N_A_LAYERS, G, C, P), P ** -0.5),
        "ssm_log_step": jax.random.uniform(ks[16], (N_A_LAYERS, G), f32,
                                           minval=math.log(STEP_MIN), maxval=math.log(STEP_MAX)),
        "ssm_d": nrm(ks[17], (N_A_LAYERS, H), 1.0),
        "ssm_w_out": nrm(ks[18], (N_A_LAYERS, H, 2 * D_MODEL), H ** -0.5),
        "kv_norm": 1.0 + nrm(ks[19], (D_MODEL,), 0.02),
        "w_kv": nrm(ks[20], (D_MODEL, 2 * N_KV_HEADS * HEAD_DIM), D_MODEL ** -0.5),
        "k_norm": 1.0 + nrm(ks[21], (HEAD_DIM,), 0.02),
        "attn_w_q": nrm(ks[22], (N_B_LAYERS, D_MODEL, N_Q_HEADS * HEAD_DIM), D_MODEL ** -0.5),
        "q_norm": 1.0 + nrm(ks[23], (N_B_LAYERS, HEAD_DIM), 0.02),
        "attn_sinks": nrm(ks[24], (N_B_LAYERS, N_Q_HEADS), 0.5),
        "attn_w_o": nrm(ks[25], (N_B_LAYERS, N_Q_HEADS * HEAD_DIM, D_MODEL), (N_Q_HEADS * HEAD_DIM) ** -0.5),
    }


def reference(x, meta_tokens, ffn1_norm, ffn1_w_gate_up, ffn1_w_down, mix_norm, ffn2_norm,
              ffn2_w_gate_up, ffn2_w_down, ssm_w_in, ssm_lambda_re, ssm_lambda_im, ssm_b_re,
              ssm_b_im, ssm_c_re, ssm_c_im, ssm_log_step, ssm_d, ssm_w_out, kv_norm, w_kv,
              k_norm, attn_w_q, q_norm, attn_sinks, attn_w_o):
    bsz = x.shape[0]
    meta = jnp.broadcast_to(meta_tokens.astype(x.dtype)[None], (bsz, N_META, D_MODEL))
    h = jnp.concatenate([meta, x], axis=1)
    k = v = None
    for layer in range(DEPTH):
        if layer == N_A_LAYERS:
            k, v = shared_kv(h, kv_norm, w_kv, k_norm)
            h = h[:, N_META:]
        h = h + 0.5 * swiglu_ffn(h, ffn1_norm[layer], ffn1_w_gate_up[layer], ffn1_w_down[layer])
        hn = rms_norm(h, mix_norm[layer])
        if layer < N_A_LAYERS:
            h = h + s5_mixer(hn, ssm_w_in[layer], ssm_lambda_re[layer], ssm_lambda_im[layer],
                             ssm_b_re[layer], ssm_b_im[layer], ssm_c_re[layer], ssm_c_im[layer],
                             ssm_log_step[layer], ssm_d[layer], ssm_w_out[layer])
        else:
            j = layer - N_A_LAYERS
            h = h + swa_sink_attention(hn, k, v, attn_w_q[j], q_norm[j], attn_sinks[j], attn_w_o[j])
        h = h + 0.5 * swiglu_ffn(h, ffn2_norm[layer], ffn2_w_gate_up[layer], ffn2_w_down[layer])
    return h
```

```python
import functools
import math

import jax
import jax.numpy as jnp
from jax import lax
from jax.experimental import pallas as pl
from jax.experimental.pallas import tpu as pltpu

F32 = jnp.float32
BF16 = jnp.bfloat16

D_MODEL = 1024
D_FF = 2816
N_META = 16
SSM_WIDTH = 512
SSM_GROUP = 16
SSM_GROUPS = 32
SSM_STATE = 64
HEAD_DIM = 64
N_Q_HEADS = 16
N_KV_HEADS = 4
Q_PER_KV = 4
KV_WIDTH = N_KV_HEADS * HEAD_DIM
BLOCK = 128
ROPE_THETA = 10000.0
EPS = 1e-6
NEG_INF = -1e30

LANES = 128
SSM_QUARTERS = 4
Q_GROUPS = SSM_GROUPS // SSM_QUARTERS
Q_IN = Q_GROUPS * SSM_GROUP
Q_STATE = Q_GROUPS * SSM_STATE
FFN_CHUNK = 1408
VMEM_LIMIT = 56 * 1024 * 1024


def _rms(x, g):
    return x * lax.rsqrt(jnp.mean(x * x, axis=-1, keepdims=True) + EPS) * g


def _const_spec(shape):
    nd = len(shape)
    return pl.BlockSpec(shape, lambda *_: (0,) * nd, pipeline_mode=pl.Buffered(1))


def _ffn_body(x_ref, g_ref, wgu_ref, wd_ref, o_ref, act_ref):
    x = x_ref[...]
    xn = _rms(x, g_ref[...]).astype(BF16)
    for c in range(D_FF // FFN_CHUNK):
        lo = c * FFN_CHUNK
        a = jnp.dot(xn, wgu_ref[:, lo:lo + FFN_CHUNK], preferred_element_type=F32)
        b = jnp.dot(xn, wgu_ref[:, D_FF + lo:D_FF + lo + FFN_CHUNK], preferred_element_type=F32)
        act_ref[:, lo:lo + FFN_CHUNK] = (a * jax.nn.sigmoid(a) * b).astype(BF16)
    y = jnp.dot(act_ref[...], wd_ref[...], preferred_element_type=F32)
    o_ref[...] = x + 0.5 * y


def _ffn_call(h2d, g, wgu, wd, tm):
    rows = h2d.shape[0]
    assert rows % tm == 0
    return pl.pallas_call(
        _ffn_body,
        grid=(rows // tm,),
        in_specs=[
            pl.BlockSpec((tm, D_MODEL), lambda i: (i, 0)),
            _const_spec((1, D_MODEL)),
            _const_spec((D_MODEL, 2 * D_FF)),
            _const_spec((D_FF, D_MODEL)),
        ],
        out_specs=pl.BlockSpec((tm, D_MODEL), lambda i: (i, 0)),
        out_shape=jax.ShapeDtypeStruct((rows, D_MODEL), F32),
        scratch_shapes=[pltpu.VMEM((tm, D_FF), BF16)],
        compiler_params=pltpu.CompilerParams(
            dimension_semantics=("arbitrary",), vmem_limit_bytes=VMEM_LIMIT),
        name="ffn",
    )(h2d, g.reshape(1, D_MODEL), wgu, wd)


def _s5_body(h_ref, g_ref, win_ref, perm_ref, permt_ref, wb_ref, lam_ref, wc_ref, d_ref, wout_ref,
             x0r_ref, x0i_ref, o_ref, xfr_ref, xfi_ref, u_tb, xs, y_tb, st_r, st_i, *, nb, tt):
    rows = nb * tt

    @pl.when(pl.program_id(0) == 0)
    def _():
        st_r[...] = x0r_ref[...]
        st_i[...] = x0i_ref[...]

    h = h_ref[...].reshape(rows, D_MODEL)
    hn = _rms(h, g_ref[...]).astype(BF16)
    u = jnp.dot(hn, win_ref[...], preferred_element_type=F32)
    u_hi = u.astype(BF16)
    u_lo = (u - u_hi.astype(F32)).astype(BF16)
    perm = perm_ref[...]
    u_tb[...] = (jnp.dot(perm, u_hi, preferred_element_type=F32)
                 + jnp.dot(perm, u_lo, preferred_element_type=F32))

    for q in range(SSM_QUARTERS):
        uq = u_tb[:, q * Q_IN:(q + 1) * Q_IN].astype(BF16)
        xs[...] = jnp.dot(uq, wb_ref[q], preferred_element_type=F32)
        sl = slice(q * Q_STATE, (q + 1) * Q_STATE)
        ar = jnp.broadcast_to(lam_ref[0:1, sl], (nb, Q_STATE))
        ai = jnp.broadcast_to(lam_ref[1:2, sl], (nb, Q_STATE))

        def step(t, carry, ar=ar, ai=ai):
            xr, xi = carry
            r0 = pl.multiple_of(t * nb, nb)
            nxr = ar * xr - ai * xi + xs[pl.ds(r0, nb), 0:Q_STATE]
            nxi = ar * xi + ai * xr + xs[pl.ds(r0, nb), Q_STATE:2 * Q_STATE]
            xs[pl.ds(r0, nb), 0:Q_STATE] = nxr
            xs[pl.ds(r0, nb), Q_STATE:2 * Q_STATE] = nxi
            return nxr, nxi

        xr, xi = lax.fori_loop(0, tt, step, (st_r[:, sl], st_i[:, sl]))
        st_r[:, sl] = xr
        st_i[:, sl] = xi
        y_tb[:, q * Q_IN:(q + 1) * Q_IN] = jnp.dot(
            xs[...].astype(BF16), wc_ref[q], preferred_element_type=F32)

    y = y_tb[...] + d_ref[...] * u_tb[...]
    y = jax.nn.gelu(y).astype(BF16)
    y_bt = jnp.dot(permt_ref[...], y, preferred_element_type=F32).astype(BF16)
    z = jnp.dot(y_bt, wout_ref[...], preferred_element_type=F32)
    out = h + z[:, :D_MODEL] * jax.nn.sigmoid(z[:, D_MODEL:])
    o_ref[...] = out.reshape(nb, tt, D_MODEL)
    xfr_ref[...] = st_r[...]
    xfi_ref[...] = st_i[...]


def _s5_call(h, g, win, perm, permt, wb, lam, wc, d, wout, x0r, x0i, tt):
    nb, seq, _ = h.shape
    rows = nb * tt
    n_state = SSM_GROUPS * SSM_STATE
    body = functools.partial(_s5_body, nb=nb, tt=tt)
    return pl.pallas_call(
        body,
        grid=(seq // tt,),
        in_specs=[
            pl.BlockSpec((nb, tt, D_MODEL), lambda i: (0, i, 0)),
            _const_spec((1, D_MODEL)),
            _const_spec((D_MODEL, SSM_WIDTH)),
            _const_spec((rows, rows)),
            _const_spec((rows, rows)),
            _const_spec((SSM_QUARTERS, Q_IN, 2 * Q_STATE)),
            _const_spec((2, n_state)),
            _const_spec((SSM_QUARTERS, 2 * Q_STATE, Q_IN)),
            _const_spec((1, SSM_WIDTH)),
            _const_spec((SSM_WIDTH, 2 * D_MODEL)),
            _const_spec((nb, n_state)),
            _const_spec((nb, n_state)),
        ],
        out_specs=[
            pl.BlockSpec((nb, tt, D_MODEL), lambda i: (0, i, 0)),
            pl.BlockSpec((nb, n_state), lambda i: (0, 0)),
            pl.BlockSpec((nb, n_state), lambda i: (0, 0)),
        ],
        out_shape=[
            jax.ShapeDtypeStruct((nb, seq, D_MODEL), F32),
            jax.ShapeDtypeStruct((nb, n_state), F32),
            jax.ShapeDtypeStruct((nb, n_state), F32),
        ],
        scratch_shapes=[
            pltpu.VMEM((rows, SSM_WIDTH), F32),
            pltpu.VMEM((rows, 2 * Q_STATE), F32),
            pltpu.VMEM((rows, SSM_WIDTH), F32),
            pltpu.VMEM((nb, n_state), F32),
            pltpu.VMEM((nb, n_state), F32),
        ],
        compiler_params=pltpu.CompilerParams(
            dimension_semantics=("arbitrary",), vmem_limit_bytes=VMEM_LIMIT),
        name="s5_mixer",
    )(h, g.reshape(1, D_MODEL), win, perm, permt, wb, lam, wc, d.reshape(1, SSM_WIDTH), wout, x0r, x0i)


def _s5_discretise(lam_re, lam_im, b_re, b_im, c_re, c_im, log_step):
    lr, li = lam_re.astype(F32), lam_im.astype(F32)
    step = jnp.exp(log_step.astype(F32))[:, None]
    mag = jnp.exp(lr * step)
    ar = mag * jnp.cos(li * step)
    ai = mag * jnp.sin(li * step)
    den = lr * lr + li * li
    nr, ni = ar - 1.0, ai
    cr = (nr * lr + ni * li) / den
    ci = (ni * lr - nr * li) / den
    br, bi = b_re.astype(F32), b_im.astype(F32)
    bbar_r = cr[..., None] * br - ci[..., None] * bi
    bbar_i = cr[..., None] * bi + ci[..., None] * br
    eye = jnp.eye(Q_GROUPS, dtype=F32)

    def expand(bb):
        t = jnp.einsum('qgpc,gh->qgchp', bb.reshape(SSM_QUARTERS, Q_GROUPS, SSM_STATE, SSM_GROUP), eye)
        return t.reshape(SSM_QUARTERS, Q_IN, Q_STATE)

    def contract(cc):
        t = jnp.einsum('qgcp,gh->qgphc', cc.reshape(SSM_QUARTERS, Q_GROUPS, SSM_GROUP, SSM_STATE), eye)
        return t.reshape(SSM_QUARTERS, Q_STATE, Q_IN)

    wb = jnp.concatenate([expand(bbar_r), expand(bbar_i)], axis=-1).astype(BF16)
    wc = jnp.concatenate([contract(c_re.astype(F32)), -contract(c_im.astype(F32))], axis=1).astype(BF16)
    lam = jnp.stack([ar.reshape(-1), ai.reshape(-1)])
    return wb, lam, wc


def _perm_matrix(nb, tt):
    r = jnp.arange(nb * tt)
    src = (r % nb) * tt + r // nb
    return (src[:, None] == jnp.arange(nb * tt)[None, :]).astype(BF16)


def _head_norm_rope(x, gain, cos, sin_signed, seg):
    ms = jnp.dot((x * x).astype(BF16), seg, preferred_element_type=F32)
    xn = x * lax.rsqrt(ms + EPS) * gain
    lane = lax.broadcasted_iota(jnp.int32, xn.shape, 1)
    first_half = (lane % HEAD_DIM) < (HEAD_DIM // 2)
    partner = jnp.where(first_half,
                        pltpu.roll(xn, LANES - HEAD_DIM // 2, 1),
                        pltpu.roll(xn, HEAD_DIM // 2, 1))
    return xn * cos + partner * sin_signed


def _rope_tables(pos):
    half = HEAD_DIM // 2
    freqs = ROPE_THETA ** (-jnp.arange(0, half, dtype=F32) * 2.0 / HEAD_DIM)
    ang = pos.astype(F32)[:, None] * freqs[None, :]
    cos, sin = jnp.cos(ang), jnp.sin(ang)
    cos_t = jnp.tile(cos, (1, 2 * LANES // HEAD_DIM))
    sin_t = jnp.tile(jnp.concatenate([-sin, sin], axis=-1), (1, LANES // HEAD_DIM))
    return cos_t, sin_t


def _seg_matrix():
    i = jnp.arange(LANES)
    return ((i[:, None] // HEAD_DIM) == (i[None, :] // HEAD_DIM)).astype(F32).astype(BF16) * (1.0 / HEAD_DIM)


def _kv_body(h_ref, g_ref, w_ref, kg_ref, cos_ref, sin_ref, seg_ref, k_ref, v_ref):
    hn = _rms(h_ref[...], g_ref[...]).astype(BF16)
    kv = jnp.dot(hn, w_ref[...], preferred_element_type=F32)
    seg = seg_ref[...]
    for j in range(KV_WIDTH // LANES):
        kt = _head_norm_rope(kv[:, j * LANES:(j + 1) * LANES], kg_ref[...], cos_ref[...], sin_ref[...], seg)
        k_ref[:, j * LANES:(j + 1) * LANES] = kt.astype(BF16)
    v_ref[...] = kv[:, KV_WIDTH:].astype(BF16)


def _kv_call(h2d, g, w_kv, kg128, cos_t, sin_t, seg, tm, seq_blocks):
    rows = h2d.shape[0]
    return pl.pallas_call(
        _kv_body,
        grid=(rows // tm,),
        in_specs=[
            pl.BlockSpec((tm, D_MODEL), lambda i: (i, 0)),
            _const_spec((1, D_MODEL)),
            _const_spec((D_MODEL, 2 * KV_WIDTH)),
            _const_spec((1, LANES)),
            pl.BlockSpec((tm, LANES), lambda i: (i % seq_blocks, 0)),
            pl.BlockSpec((tm, LANES), lambda i: (i % seq_blocks, 0)),
            _const_spec((LANES, LANES)),
        ],
        out_specs=[
            pl.BlockSpec((tm, KV_WIDTH), lambda i: (i, 0)),
            pl.BlockSpec((tm, KV_WIDTH), lambda i: (i, 0)),
        ],
        out_shape=[
            jax.ShapeDtypeStruct((rows, KV_WIDTH), BF16),
            jax.ShapeDtypeStruct((rows, KV_WIDTH), BF16),
        ],
        compiler_params=pltpu.CompilerParams(
            dimension_semantics=("arbitrary",), vmem_limit_bytes=VMEM_LIMIT),
        name="shared_kv",
    )(h2d, g.reshape(1, D_MODEL), w_kv, kg128, cos_t, sin_t, seg)


def _attn_body(sink_ref, h_ref, g_ref, wq_ref, qg_ref, cos_ref, sin_ref, seg_ref,
               kp_ref, kc_ref, vp_ref, vc_ref, km_ref, vm_ref, wo_ref, o_ref, q_scr, o_scr):
    n = pl.program_id(1)
    h = h_ref[0]
    hn = _rms(h, g_ref[...]).astype(BF16)
    q = jnp.dot(hn, wq_ref[...], preferred_element_type=F32)
    seg = seg_ref[...]
    scale = HEAD_DIM ** -0.5
    for j in range(D_MODEL // LANES):
        qt = _head_norm_rope(q[:, j * LANES:(j + 1) * LANES], qg_ref[...], cos_ref[...], sin_ref[...], seg)
        q_scr[:, j * LANES:(j + 1) * LANES] = (qt * scale).astype(BF16)

    k_band = jnp.concatenate([kp_ref[0], kc_ref[0]], axis=0)
    v_band = jnp.concatenate([vp_ref[0], vc_ref[0]], axis=0)
    k_meta = km_ref[...]
    v_meta = vm_ref[...]
    qi = lax.broadcasted_iota(jnp.int32, (BLOCK, 2 * BLOCK), 0)
    kj = lax.broadcasted_iota(jnp.int32, (BLOCK, 2 * BLOCK), 1)
    rel = qi + BLOCK - kj
    valid = (rel >= 0) & (rel < BLOCK) & ((n > 0) | (kj >= BLOCK))
    nt = (((1,), (1,)), ((), ()))
    for hd in range(N_Q_HEADS):
        kv0 = (hd // Q_PER_KV) * HEAD_DIM
        qh = q_scr[:, hd * HEAD_DIM:(hd + 1) * HEAD_DIM]
        s_b = lax.dot_general(qh, k_band[:, kv0:kv0 + HEAD_DIM], nt, preferred_element_type=F32)
        s_m = lax.dot_general(qh, k_meta[:, kv0:kv0 + HEAD_DIM], nt, preferred_element_type=F32)
        s_b = jnp.where(valid, s_b, NEG_INF)
        sink = sink_ref[hd]
        m = jnp.maximum(jnp.maximum(jnp.max(s_b, axis=-1, keepdims=True),
                                    jnp.max(s_m, axis=-1, keepdims=True)), sink)
        p_b = jnp.exp(s_b - m)
        p_m = jnp.exp(s_m - m)
        den = (jnp.sum(p_b, axis=-1, keepdims=True) + jnp.sum(p_m, axis=-1, keepdims=True)
               + jnp.exp(sink - m))
        o = (jnp.dot(p_b.astype(BF16), v_band[:, kv0:kv0 + HEAD_DIM], preferred_element_type=F32)
             + jnp.dot(p_m.astype(BF16), v_meta[:, kv0:kv0 + HEAD_DIM], preferred_element_type=F32))
        o_scr[:, hd * HEAD_DIM:(hd + 1) * HEAD_DIM] = (o / den).astype(BF16)
    o_ref[0] = h + jnp.dot(o_scr[...], wo_ref[...], preferred_element_type=F32)


def _attn_call(h, g, wq, qg128, cos_t, sin_t, seg, k, v, k_meta, v_meta, sinks, wo):
    nb, seq, _ = h.shape
    nblk = seq // BLOCK
    kv_cur = pl.BlockSpec((1, BLOCK, KV_WIDTH), lambda b, n, *_: (b, n, 0))
    kv_prev = pl.BlockSpec((1, BLOCK, KV_WIDTH), lambda b, n, *_: (b, jnp.maximum(n - 1, 0), 0))

    def const(shape):
        nd = len(shape)
        return pl.BlockSpec(shape, lambda *_: (0,) * nd, pipeline_mode=pl.Buffered(1))

    grid_spec = pltpu.PrefetchScalarGridSpec(
        num_scalar_prefetch=1,
        grid=(nb, nblk),
        in_specs=[
            pl.BlockSpec((1, BLOCK, D_MODEL), lambda b, n, *_: (b, n, 0)),
            const((1, D_MODEL)),
            const((D_MODEL, D_MODEL)),
            const((1, LANES)),
            pl.BlockSpec((BLOCK, LANES), lambda b, n, *_: (n, 0)),
            pl.BlockSpec((BLOCK, LANES), lambda b, n, *_: (n, 0)),
            const((LANES, LANES)),
            kv_prev, kv_cur, kv_prev, kv_cur,
            const((N_META, KV_WIDTH)),
            const((N_META, KV_WIDTH)),
            const((D_MODEL, D_MODEL)),
        ],
        out_specs=pl.BlockSpec((1, BLOCK, D_MODEL), lambda b, n, *_: (b, n, 0)),
        scratch_shapes=[pltpu.VMEM((BLOCK, D_MODEL), BF16), pltpu.VMEM((BLOCK, D_MODEL), BF16)],
    )
    return pl.pallas_call(
        _attn_body,
        grid_spec=grid_spec,
        out_shape=jax.ShapeDtypeStruct((nb, seq, D_MODEL), F32),
        compiler_params=pltpu.CompilerParams(
            dimension_semantics=("arbitrary", "arbitrary"), vmem_limit_bytes=VMEM_LIMIT),
        name="swa_attention",
    )(sinks, h, g.reshape(1, D_MODEL), wq, qg128, cos_t, sin_t, seg, k, k, v, v, k_meta, v_meta, wo)


def kernel(x, meta_tokens, ffn1_norm, ffn1_w_gate_up, ffn1_w_down, mix_norm, ffn2_norm, ffn2_w_gate_up,
           ffn2_w_down, ssm_w_in, ssm_lambda_re, ssm_lambda_im, ssm_b_re, ssm_b_im, ssm_c_re, ssm_c_im,
           ssm_log_step, ssm_d, ssm_w_out, kv_norm, w_kv, k_norm, attn_w_q, q_norm, attn_sinks, attn_w_o):
    nb, seq, _ = x.shape
    rows = nb * seq
    tm = 512
    tt = 64
    assert seq % tm == 0 and seq % tt == 0 and seq % BLOCK == 0
    bf = lambda w: w.astype(BF16)
    n_state = SSM_GROUPS * SSM_STATE

    wb, lam, wc = _s5_discretise(ssm_lambda_re[0], ssm_lambda_im[0], ssm_b_re[0], ssm_b_im[0],
                                 ssm_c_re[0], ssm_c_im[0], ssm_log_step[0])
    s5_w = (mix_norm[0], bf(ssm_w_in[0]))
    s5_w2 = (wb, lam, wc, ssm_d[0].astype(F32), bf(ssm_w_out[0]))
    ffn = lambda h2d, which, layer, t: _ffn_call(
        h2d, (ffn1_norm, ffn2_norm)[which][layer],
        bf((ffn1_w_gate_up, ffn2_w_gate_up)[which][layer]),
        bf((ffn1_w_down, ffn2_w_down)[which][layer]), t)
    seg = _seg_matrix()
    w_kv_b = bf(w_kv)
    kg128 = jnp.tile(k_norm.astype(F32), LANES // HEAD_DIM).reshape(1, LANES)

    hm = ffn(meta_tokens.astype(F32), 0, 0, N_META)
    hm_b = jnp.broadcast_to(hm[None], (nb, N_META, D_MODEL))
    zeros = jnp.zeros((nb, n_state), F32)
    perm_m = _perm_matrix(nb, N_META)
    hm_b, x0r, x0i = _s5_call(hm_b, *s5_w, perm_m, perm_m.T, *s5_w2, zeros, zeros, N_META)
    hm = ffn(hm_b[0], 1, 0, N_META)
    cos_m, sin_m = _rope_tables(jnp.arange(N_META))
    k_meta, v_meta = _kv_call(hm, kv_norm, w_kv_b, kg128, cos_m, sin_m, seg, N_META, 1)

    h = ffn(x.reshape(rows, D_MODEL), 0, 0, tm)
    perm = _perm_matrix(nb, tt)
    h, _, _ = _s5_call(h.reshape(nb, seq, D_MODEL), *s5_w, perm, perm.T, *s5_w2, x0r, x0i, tt)
    h = ffn(h.reshape(rows, D_MODEL), 1, 0, tm)

    cos_t, sin_t = _rope_tables(N_META + jnp.arange(seq))
    k, v = _kv_call(h, kv_norm, w_kv_b, kg128, cos_t, sin_t, seg, tm, seq // tm)
    k = k.reshape(nb, seq, KV_WIDTH)
    v = v.reshape(nb, seq, KV_WIDTH)

    h = ffn(h, 0, 1, tm)
    qg128 = jnp.tile(q_norm[0].astype(F32), LANES // HEAD_DIM).reshape(1, LANES)
    h = _attn_call(h.reshape(nb, seq, D_MODEL), mix_norm[1], bf(attn_w_q[0]), qg128, cos_t, sin_t, seg,
                   k, v, k_meta, v_meta, attn_sinks[0].astype(F32), bf(attn_w_o[0]))
    h = ffn(h.reshape(rows, D_MODEL), 1, 1, tm)
    return h.reshape(nb, seq, D_MODEL)
```

```python
import functools
import math

import jax
import jax.numpy as jnp
from jax import lax
from jax.experimental import pallas as pl
from jax.experimental.pallas import tpu as pltpu

F32 = jnp.float32
BF16 = jnp.bfloat16

D_MODEL = 1024
D_FF = 2816
N_META = 16
SSM_WIDTH = 512
SSM_GROUP = 16
SSM_GROUPS = 32
SSM_STATE = 64
HEAD_DIM = 64
N_Q_HEADS = 16
N_KV_HEADS = 4
Q_PER_KV = 4
KV_WIDTH = N_KV_HEADS * HEAD_DIM
BLOCK = 128
ROPE_THETA = 10000.0
EPS = 1e-6
NEG_INF = -1e30

LANES = 128
SSM_QUARTERS = 4
Q_GROUPS = SSM_GROUPS // SSM_QUARTERS
Q_IN = Q_GROUPS * SSM_GROUP
Q_STATE = Q_GROUPS * SSM_STATE
FFN_CHUNK = 1408
VMEM_LIMIT = 56 * 1024 * 1024


def _rms(x, g):
    return x * lax.rsqrt(jnp.mean(x * x, axis=-1, keepdims=True) + EPS) * g


def _const_spec(shape):
    nd = len(shape)
    return pl.BlockSpec(shape, lambda *_: (0,) * nd, pipeline_mode=pl.Buffered(1))


def _ffn_body(x_ref, g_ref, wgu_ref, wd_ref, o_ref, act_ref):
    x = x_ref[...]
    xn = _rms(x, g_ref[...]).astype(BF16)
    for c in range(D_FF // FFN_CHUNK):
        lo = c * FFN_CHUNK
        a = jnp.dot(xn, wgu_ref[:, lo:lo + FFN_CHUNK], preferred_element_type=F32)
        b = jnp.dot(xn, wgu_ref[:, D_FF + lo:D_FF + lo + FFN_CHUNK], preferred_element_type=F32)
        act_ref[:, lo:lo + FFN_CHUNK] = (a * jax.nn.sigmoid(a) * b).astype(BF16)
    y = jnp.dot(act_ref[...], wd_ref[...], preferred_element_type=F32)
    o_ref[...] = x + 0.5 * y


def _ffn_call(h2d, g, wgu, wd, tm):
    rows = h2d.shape[0]
    assert rows % tm == 0
    return pl.pallas_call(
        _ffn_body,
        grid=(rows // tm,),
        in_specs=[
            pl.BlockSpec((tm, D_MODEL), lambda i: (i, 0)),
            _const_spec((1, D_MODEL)),
            _const_spec((D_MODEL, 2 * D_FF)),
            _const_spec((D_FF, D_MODEL)),
        ],
        out_specs=pl.BlockSpec((tm, D_MODEL), lambda i: (i, 0)),
        out_shape=jax.ShapeDtypeStruct((rows, D_MODEL), F32),
        scratch_shapes=[pltpu.VMEM((tm, D_FF), BF16)],
        compiler_params=pltpu.CompilerParams(
            dimension_semantics=("arbitrary",), vmem_limit_bytes=VMEM_LIMIT),
        name="ffn",
    )(h2d, g.reshape(1, D_MODEL), wgu, wd)


def _s5_pitch(tt):
    return tt + 8


def _s5_body(h_ref, g_ref, win_ref, wb_ref, lam_ref, wc_ref, d_ref, wout_ref,
             x0r_ref, x0i_ref, o_ref, xfr_ref, xfi_ref, pit, u_tb, xs, y_tb, y_bt, st_r, st_i, *, nb, tt):
    rows = nb * tt
    pitch = _s5_pitch(tt)
    tiles = SSM_WIDTH // LANES

    @pl.when(pl.program_id(0) == 0)
    def _():
        st_r[...] = x0r_ref[...]
        st_i[...] = x0i_ref[...]

    h = h_ref[...].reshape(rows, D_MODEL)
    hn = _rms(h, g_ref[...]).astype(BF16)
    u = jnp.dot(hn, win_ref[...], preferred_element_type=F32)
    for b in range(nb):
        for j in range(tiles):
            pit[j, b * pitch:b * pitch + tt, :] = u[b * tt:(b + 1) * tt, j * LANES:(j + 1) * LANES]
    for t in range(tt):
        for j in range(tiles):
            u_tb[t * nb:(t + 1) * nb, j * LANES:(j + 1) * LANES] = pit[j, pl.ds(t, nb, stride=pitch), :]

    for q in range(SSM_QUARTERS):
        uq = u_tb[:, q * Q_IN:(q + 1) * Q_IN].astype(BF16)
        xs[...] = jnp.dot(uq, wb_ref[q], preferred_element_type=F32)
        sl = slice(q * Q_STATE, (q + 1) * Q_STATE)
        ar = jnp.broadcast_to(lam_ref[0:1, sl], (nb, Q_STATE))
        ai = jnp.broadcast_to(lam_ref[1:2, sl], (nb, Q_STATE))

        def step(t, carry, ar=ar, ai=ai):
            xr, xi = carry
            r0 = pl.multiple_of(t * nb, nb)
            nxr = ar * xr - ai * xi + xs[pl.ds(r0, nb), 0:Q_STATE]
            nxi = ar * xi + ai * xr + xs[pl.ds(r0, nb), Q_STATE:2 * Q_STATE]
            xs[pl.ds(r0, nb), 0:Q_STATE] = nxr
            xs[pl.ds(r0, nb), Q_STATE:2 * Q_STATE] = nxi
            return nxr, nxi

        xr, xi = lax.fori_loop(0, tt, step, (st_r[:, sl], st_i[:, sl]))
        st_r[:, sl] = xr
        st_i[:, sl] = xi
        y_tb[:, q * Q_IN:(q + 1) * Q_IN] = jnp.dot(
            xs[...].astype(BF16), wc_ref[q], preferred_element_type=F32)

    d = d_ref[...]
    for t in range(tt):
        r = slice(t * nb, (t + 1) * nb)
        y = jax.nn.gelu(y_tb[r, :] + d * u_tb[r, :])
        for j in range(tiles):
            pit[j, pl.ds(t, nb, stride=pitch), :] = y[:, j * LANES:(j + 1) * LANES]
    for b in range(nb):
        for j in range(tiles):
            y_bt[b * tt:(b + 1) * tt, j * LANES:(j + 1) * LANES] = pit[j, b * pitch:b * pitch + tt, :].astype(BF16)
    z = jnp.dot(y_bt[...], wout_ref[...], preferred_element_type=F32)
    out = h + z[:, :D_MODEL] * jax.nn.sigmoid(z[:, D_MODEL:])
    o_ref[...] = out.reshape(nb, tt, D_MODEL)
    xfr_ref[...] = st_r[...]
    xfi_ref[...] = st_i[...]


def _s5_call(h, g, win, wb, lam, wc, d, wout, x0r, x0i, tt):
    nb, seq, _ = h.shape
    rows = nb * tt
    assert tt % 8 == 0 and nb % 8 == 0
    n_state = SSM_GROUPS * SSM_STATE
    body = functools.partial(_s5_body, nb=nb, tt=tt)
    return pl.pallas_call(
        body,
        grid=(seq // tt,),
        in_specs=[
            pl.BlockSpec((nb, tt, D_MODEL), lambda i: (0, i, 0)),
            _const_spec((1, D_MODEL)),
            _const_spec((D_MODEL, SSM_WIDTH)),
            _const_spec((SSM_QUARTERS, Q_IN, 2 * Q_STATE)),
            _const_spec((2, n_state)),
            _const_spec((SSM_QUARTERS, 2 * Q_STATE, Q_IN)),
            _const_spec((1, SSM_WIDTH)),
            _const_spec((SSM_WIDTH, 2 * D_MODEL)),
            _const_spec((nb, n_state)),
            _const_spec((nb, n_state)),
        ],
        out_specs=[
            pl.BlockSpec((nb, tt, D_MODEL), lambda i: (0, i, 0)),
            pl.BlockSpec((nb, n_state), lambda i: (0, 0)),
            pl.BlockSpec((nb, n_state), lambda i: (0, 0)),
        ],
        out_shape=[
            jax.ShapeDtypeStruct((nb, seq, D_MODEL), F32),
            jax.ShapeDtypeStruct((nb, n_state), F32),
            jax.ShapeDtypeStruct((nb, n_state), F32),
        ],
        scratch_shapes=[
            pltpu.VMEM((SSM_WIDTH // LANES, nb * _s5_pitch(tt), LANES), F32),
            pltpu.VMEM((rows, SSM_WIDTH), F32),
            pltpu.VMEM((rows, 2 * Q_STATE), F32),
            pltpu.VMEM((rows, SSM_WIDTH), F32),
            pltpu.VMEM((rows, SSM_WIDTH), BF16),
            pltpu.VMEM((nb, n_state), F32),
            pltpu.VMEM((nb, n_state), F32),
        ],
        compiler_params=pltpu.CompilerParams(
            dimension_semantics=("arbitrary",), vmem_limit_bytes=VMEM_LIMIT),
        name="s5_mixer",
    )(h, g.reshape(1, D_MODEL), win, wb, lam, wc, d.reshape(1, SSM_WIDTH), wout, x0r, x0i)


def _s5_discretise(lam_re, lam_im, b_re, b_im, c_re, c_im, log_step):
    lr, li = lam_re.astype(F32), lam_im.astype(F32)
    step = jnp.exp(log_step.astype(F32))[:, None]
    mag = jnp.exp(lr * step)
    ar = mag * jnp.cos(li * step)
    ai = mag * jnp.sin(li * step)
    den = lr * lr + li * li
    nr, ni = ar - 1.0, ai
    cr = (nr * lr + ni * li) / den
    ci = (ni * lr - nr * li) / den
    br, bi = b_re.astype(F32), b_im.astype(F32)
    bbar_r = cr[..., None] * br - ci[..., None] * bi
    bbar_i = cr[..., None] * bi + ci[..., None] * br
    eye = jnp.eye(Q_GROUPS, dtype=F32)

    def expand(bb):
        t = jnp.einsum('qgpc,gh->qgchp', bb.reshape(SSM_QUARTERS, Q_GROUPS, SSM_STATE, SSM_GROUP), eye)
        return t.reshape(SSM_QUARTERS, Q_IN, Q_STATE)

    def contract(cc):
        t = jnp.einsum('qgcp,gh->qgphc', cc.reshape(SSM_QUARTERS, Q_GROUPS, SSM_GROUP, SSM_STATE), eye)
        return t.reshape(SSM_QUARTERS, Q_STATE, Q_IN)

    wb = jnp.concatenate([expand(bbar_r), expand(bbar_i)], axis=-1).astype(BF16)
    wc = jnp.concatenate([contract(c_re.astype(F32)), -contract(c_im.astype(F32))], axis=1).astype(BF16)
    lam = jnp.stack([ar.reshape(-1), ai.reshape(-1)])
    return wb, lam, wc


def _head_norm_rope(x, gain, cos, sin_signed, seg):
    ms = jnp.dot((x * x).astype(BF16), seg, preferred_element_type=F32)
    xn = x * lax.rsqrt(ms + EPS) * gain
    lane = lax.broadcasted_iota(jnp.int32, xn.shape, 1)
    first_half = (lane % HEAD_DIM) < (HEAD_DIM // 2)
    partner = jnp.where(first_half,
                        pltpu.roll(xn, LANES - HEAD_DIM // 2, 1),
                        pltpu.roll(xn, HEAD_DIM // 2, 1))
    return xn * cos + partner * sin_signed


def _rope_tables(pos):
    half = HEAD_DIM // 2
    freqs = ROPE_THETA ** (-jnp.arange(0, half, dtype=F32) * 2.0 / HEAD_DIM)
    ang = pos.astype(F32)[:, None] * freqs[None, :]
    cos, sin = jnp.cos(ang), jnp.sin(ang)
    cos_t = jnp.tile(cos, (1, 2 * LANES // HEAD_DIM))
    sin_t = jnp.tile(jnp.concatenate([-sin, sin], axis=-1), (1, LANES // HEAD_DIM))
    return cos_t, sin_t


def _seg_matrix():
    i = jnp.arange(LANES)
    return ((i[:, None] // HEAD_DIM) == (i[None, :] // HEAD_DIM)).astype(F32).astype(BF16) * (1.0 / HEAD_DIM)


def _kv_body(h_ref, g_ref, w_ref, kg_ref, cos_ref, sin_ref, seg_ref, k_ref, v_ref, *, transpose_v):
    hn = _rms(h_ref[...], g_ref[...]).astype(BF16)
    kv = jnp.dot(hn, w_ref[...], preferred_element_type=F32)
    seg = seg_ref[...]
    for j in range(KV_WIDTH // LANES):
        kt = _head_norm_rope(kv[:, j * LANES:(j + 1) * LANES], kg_ref[...], cos_ref[...], sin_ref[...], seg)
        k_ref[:, j * LANES:(j + 1) * LANES] = kt.astype(BF16)
    if transpose_v:
        v_ref[0] = kv[:, KV_WIDTH:].T.astype(BF16)
    else:
        v_ref[...] = kv[:, KV_WIDTH:].astype(BF16)


def _kv_call(h2d, g, w_kv, kg128, cos_t, sin_t, seg, tm, seq_blocks, transpose_v):
    rows = h2d.shape[0]
    if transpose_v:
        v_spec = pl.BlockSpec((1, KV_WIDTH, tm), lambda i: (i // seq_blocks, 0, i % seq_blocks))
        v_shape = jax.ShapeDtypeStruct((rows // (tm * seq_blocks), KV_WIDTH, tm * seq_blocks), BF16)
    else:
        v_spec = pl.BlockSpec((tm, KV_WIDTH), lambda i: (i, 0))
        v_shape = jax.ShapeDtypeStruct((rows, KV_WIDTH), BF16)
    return pl.pallas_call(
        functools.partial(_kv_body, transpose_v=transpose_v),
        grid=(rows // tm,),
        in_specs=[
            pl.BlockSpec((tm, D_MODEL), lambda i: (i, 0)),
            _const_spec((1, D_MODEL)),
            _const_spec((D_MODEL, 2 * KV_WIDTH)),
            _const_spec((1, LANES)),
            pl.BlockSpec((tm, LANES), lambda i: (i % seq_blocks, 0)),
            pl.BlockSpec((tm, LANES), lambda i: (i % seq_blocks, 0)),
            _const_spec((LANES, LANES)),
        ],
        out_specs=[pl.BlockSpec((tm, KV_WIDTH), lambda i: (i, 0)), v_spec],
        out_shape=[jax.ShapeDtypeStruct((rows, KV_WIDTH), BF16), v_shape],
        compiler_params=pltpu.CompilerParams(
            dimension_semantics=("arbitrary",), vmem_limit_bytes=VMEM_LIMIT),
        name="shared_kv",
    )(h2d, g.reshape(1, D_MODEL), w_kv, kg128, cos_t, sin_t, seg)


HEAD_PAIRS = N_Q_HEADS // 2


def _attn_body(h_ref, g_ref, wq_ref, qg_ref, cos_ref, sin_ref, bias_ref, sink_ref,
               kp_ref, kc_ref, km_ref, vp_ref, vc_ref, vm_ref, wo_ref, o_ref, qp_scr, ot_scr):
    @pl.when((pl.program_id(0) == 0) & (pl.program_id(1) == 0))
    def _():
        qp_scr[...] = jnp.zeros_like(qp_scr)

    h = h_ref[0]
    hn = _rms(h, g_ref[...]).astype(BF16)
    qt = jnp.dot(hn, wq_ref[...], preferred_element_type=F32).T
    cos, sin, gain = cos_ref[...], sin_ref[...], qg_ref[...]
    half = HEAD_DIM // 2
    for hd in range(N_Q_HEADS):
        x = qt[hd * HEAD_DIM:(hd + 1) * HEAD_DIM, :]
        xn = x * lax.rsqrt(jnp.mean(x * x, axis=0, keepdims=True) + EPS) * gain
        x1, x2 = xn[:half], xn[half:]
        r = jnp.concatenate([x1 * cos - x2 * sin, x2 * cos + x1 * sin], axis=0).astype(BF16)
        row0 = ((hd // Q_PER_KV) % 2) * HEAD_DIM
        qp_scr[hd // 2, row0:row0 + HEAD_DIM, (hd % 2) * LANES:(hd % 2 + 1) * LANES] = r

    for p in range(HEAD_PAIRS):
        hk = p // (Q_PER_KV // 2)
        lane0 = (hk // 2) * LANES
        rhs = qp_scr[p]
        s_p = jnp.dot(kp_ref[0, :, lane0:lane0 + LANES], rhs, preferred_element_type=F32) + bias_ref[0, :BLOCK]
        s_c = jnp.dot(kc_ref[0, :, lane0:lane0 + LANES], rhs, preferred_element_type=F32) + bias_ref[0, BLOCK:]
        s_m = jnp.dot(km_ref[:, lane0:lane0 + LANES], rhs, preferred_element_type=F32)
        sink = sink_ref[p:p + 1, :]
        m = jnp.maximum(
            jnp.maximum(jnp.max(s_p, axis=0, keepdims=True), jnp.max(s_c, axis=0, keepdims=True)),
            jnp.maximum(jnp.max(s_m, axis=0, keepdims=True), sink))
        e_p = jnp.exp(s_p - m)
        e_c = jnp.exp(s_c - m)
        e_m = jnp.exp(s_m - m)
        den = (jnp.sum(e_p, axis=0, keepdims=True) + jnp.sum(e_c, axis=0, keepdims=True)
               + jnp.sum(e_m, axis=0, keepdims=True) + jnp.exp(sink - m))
        v0 = hk * HEAD_DIM
        ot = (jnp.dot(vp_ref[0, v0:v0 + HEAD_DIM, :], e_p.astype(BF16), preferred_element_type=F32)
              + jnp.dot(vc_ref[0, v0:v0 + HEAD_DIM, :], e_c.astype(BF16), preferred_element_type=F32)
              + jnp.dot(vm_ref[v0:v0 + HEAD_DIM, :], e_m.astype(BF16), preferred_element_type=F32))
        ot = ot * (1.0 / den)
        ot_scr[2 * p * HEAD_DIM:(2 * p + 1) * HEAD_DIM, :] = ot[:, :LANES]
        ot_scr[(2 * p + 1) * HEAD_DIM:(2 * p + 2) * HEAD_DIM, :] = ot[:, LANES:]
    o = ot_scr[...].T.astype(BF16)
    o_ref[0] = h + jnp.dot(o, wo_ref[...], preferred_element_type=F32)


def _attn_tables(q_gain, sinks, pos):
    half = HEAD_DIM // 2
    freqs = ROPE_THETA ** (-jnp.arange(0, half, dtype=F32) * 2.0 / HEAD_DIM)
    ang = pos.astype(F32)[:, None] * freqs[None, :]
    cos_t, sin_t = jnp.cos(ang).T, jnp.sin(ang).T
    gain = jnp.broadcast_to((q_gain.astype(F32) * HEAD_DIM ** -0.5)[:, None], (HEAD_DIM, LANES))
    kj = jnp.arange(BLOCK)[:, None]
    qi = jnp.arange(BLOCK)[None, :]
    neg = jnp.full((BLOCK, BLOCK), NEG_INF, F32)
    prev = jnp.where(kj > qi, 0.0, NEG_INF).astype(F32)
    cur = jnp.where(kj <= qi, 0.0, NEG_INF).astype(F32)
    bias = jnp.stack([jnp.concatenate([neg, cur]), jnp.concatenate([prev, cur])])
    bias = jnp.tile(bias, (1, 1, 2))
    sink_t = jnp.repeat(sinks.astype(F32).reshape(HEAD_PAIRS, 2), LANES, axis=1)
    return cos_t, sin_t, gain, bias, sink_t


def _attn_call(h, g, wq, tables, k, vt, k_meta, vt_meta, wo):
    nb, seq, _ = h.shape
    nblk = seq // BLOCK
    cos_t, sin_t, gain, bias, sink_t = tables
    prev = lambda n: jnp.maximum(n - 1, 0)
    half = HEAD_DIM // 2
    return pl.pallas_call(
        _attn_body,
        grid=(nb, nblk),
        in_specs=[
            pl.BlockSpec((1, BLOCK, D_MODEL), lambda b, n: (b, n, 0)),
            _const_spec((1, D_MODEL)),
            _const_spec((D_MODEL, D_MODEL)),
            _const_spec((HEAD_DIM, LANES)),
            pl.BlockSpec((half, BLOCK), lambda b, n: (0, n)),
            pl.BlockSpec((half, BLOCK), lambda b, n: (0, n)),
            pl.BlockSpec((1, 2 * BLOCK, 2 * LANES), lambda b, n: (jnp.minimum(n, 1), 0, 0)),
            _const_spec((HEAD_PAIRS, 2 * LANES)),
            pl.BlockSpec((1, BLOCK, KV_WIDTH), lambda b, n: (b, prev(n), 0)),
            pl.BlockSpec((1, BLOCK, KV_WIDTH), lambda b, n: (b, n, 0)),
            _const_spec((N_META, KV_WIDTH)),
            pl.BlockSpec((1, KV_WIDTH, BLOCK), lambda b, n: (b, 0, prev(n))),
            pl.BlockSpec((1, KV_WIDTH, BLOCK), lambda b, n: (b, 0, n)),
            _const_spec((KV_WIDTH, N_META)),
            _const_spec((D_MODEL, D_MODEL)),
        ],
        out_specs=pl.BlockSpec((1, BLOCK, D_MODEL), lambda b, n: (b, n, 0)),
        out_shape=jax.ShapeDtypeStruct((nb, seq, D_MODEL), F32),
        scratch_shapes=[pltpu.VMEM((HEAD_PAIRS, LANES, 2 * LANES), BF16),
                        pltpu.VMEM((D_MODEL, BLOCK), F32)],
        compiler_params=pltpu.CompilerParams(
            dimension_semantics=("arbitrary", "arbitrary"), vmem_limit_bytes=VMEM_LIMIT),
        name="swa_attention",
    )(h, g.reshape(1, D_MODEL), wq, gain, cos_t, sin_t, bias, sink_t, k, k, k_meta, vt, vt, vt_meta, wo)


def kernel(x, meta_tokens, ffn1_norm, ffn1_w_gate_up, ffn1_w_down, mix_norm, ffn2_norm, ffn2_w_gate_up,
           ffn2_w_down, ssm_w_in, ssm_lambda_re, ssm_lambda_im, ssm_b_re, ssm_b_im, ssm_c_re, ssm_c_im,
           ssm_log_step, ssm_d, ssm_w_out, kv_norm, w_kv, k_norm, attn_w_q, q_norm, attn_sinks, attn_w_o):
    nb, seq, _ = x.shape
    rows = nb * seq
    tm = 512
    tt = 64
    assert seq % tm == 0 and seq % tt == 0 and seq % BLOCK == 0
    bf = lambda w: w.astype(BF16)
    n_state = SSM_GROUPS * SSM_STATE

    wb, lam, wc = _s5_discretise(ssm_lambda_re[0], ssm_lambda_im[0], ssm_b_re[0], ssm_b_im[0],
                                 ssm_c_re[0], ssm_c_im[0], ssm_log_step[0])
    s5_w = (mix_norm[0], bf(ssm_w_in[0]))
    s5_w2 = (wb, lam, wc, ssm_d[0].astype(F32), bf(ssm_w_out[0]))
    ffn = lambda h2d, which, layer, t: _ffn_call(
        h2d, (ffn1_norm, ffn2_norm)[which][layer],
        bf((ffn1_w_gate_up, ffn2_w_gate_up)[which][layer]),
        bf((ffn1_w_down, ffn2_w_down)[which][layer]), t)
    seg = _seg_matrix()
    w_kv_b = bf(w_kv)
    kg128 = jnp.tile(k_norm.astype(F32), LANES // HEAD_DIM).reshape(1, LANES)

    hm = ffn(meta_tokens.astype(F32), 0, 0, N_META)
    hm_b = jnp.broadcast_to(hm[None], (nb, N_META, D_MODEL))
    zeros = jnp.zeros((nb, n_state), F32)
    hm_b, x0r, x0i = _s5_call(hm_b, *s5_w, *s5_w2, zeros, zeros, N_META)
    hm = ffn(hm_b[0], 1, 0, N_META)
    cos_m, sin_m = _rope_tables(jnp.arange(N_META))
    k_meta, v_meta = _kv_call(hm, kv_norm, w_kv_b, kg128, cos_m, sin_m, seg, N_META, 1, False)

    h = ffn(x.reshape(rows, D_MODEL), 0, 0, tm)
    h, _, _ = _s5_call(h.reshape(nb, seq, D_MODEL), *s5_w, *s5_w2, x0r, x0i, tt)
    h = ffn(h.reshape(rows, D_MODEL), 1, 0, tm)

    cos_t, sin_t = _rope_tables(N_META + jnp.arange(seq))
    k, vt = _kv_call(h, kv_norm, w_kv_b, kg128, cos_t, sin_t, seg, tm, seq // tm, True)
    k = k.reshape(nb, seq, KV_WIDTH)

    h = ffn(h, 0, 1, tm)
    tables = _attn_tables(q_norm[0], attn_sinks[0], N_META + jnp.arange(seq))
    h = _attn_call(h.reshape(nb, seq, D_MODEL), mix_norm[1], bf(attn_w_q[0]), tables,
                   k, vt, k_meta, v_meta.T, bf(attn_w_o[0]))
    h = ffn(h.reshape(rows, D_MODEL), 1, 1, tm)
    return h.reshape(nb, seq, D_MODEL)
```

```python
import functools
import math

import jax
import jax.numpy as jnp
from jax import lax
from jax.experimental import pallas as pl
from jax.experimental.pallas import tpu as pltpu

F32 = jnp.float32
BF16 = jnp.bfloat16

D_MODEL = 1024
D_FF = 2816
N_META = 16
SSM_WIDTH = 512
SSM_GROUP = 16
SSM_GROUPS = 32
SSM_STATE = 64
HEAD_DIM = 64
N_Q_HEADS = 16
N_KV_HEADS = 4
Q_PER_KV = 4
KV_WIDTH = N_KV_HEADS * HEAD_DIM
BLOCK = 128
ROPE_THETA = 10000.0
EPS = 1e-6
NEG_INF = -1e30

LANES = 128
SSM_QUARTERS = 4
Q_GROUPS = SSM_GROUPS // SSM_QUARTERS
Q_IN = Q_GROUPS * SSM_GROUP
Q_STATE = Q_GROUPS * SSM_STATE
MXU_WIDTH = 256
FFN_CHUNK = MXU_WIDTH
VMEM_LIMIT = 56 * 1024 * 1024


def _rms(x, g):
    return x * lax.rsqrt(jnp.mean(x * x, axis=-1, keepdims=True) + EPS) * g


def _const_spec(shape):
    nd = len(shape)
    return pl.BlockSpec(shape, lambda *_: (0,) * nd, pipeline_mode=pl.Buffered(1))


def _ffn_body(x_ref, g_ref, wgu_ref, wd_ref, o_ref, act_ref):
    x = x_ref[...]
    xn = _rms(x, g_ref[...]).astype(BF16)
    for c in range(D_FF // FFN_CHUNK):
        lo = c * FFN_CHUNK
        a = jnp.dot(xn, wgu_ref[:, lo:lo + FFN_CHUNK], preferred_element_type=F32)
        b = jnp.dot(xn, wgu_ref[:, D_FF + lo:D_FF + lo + FFN_CHUNK], preferred_element_type=F32)
        act_ref[:, lo:lo + FFN_CHUNK] = (a * jax.nn.sigmoid(a) * b).astype(BF16)
    y = jnp.dot(act_ref[...], wd_ref[...], preferred_element_type=F32)
    o_ref[...] = x + 0.5 * y


def _ffn_call(h2d, g, wgu, wd, layer, tm):
    rows = h2d.shape[0]
    assert rows % tm == 0
    layer_spec = lambda shape: pl.BlockSpec((None,) + shape, lambda i: (layer, 0, 0),
                                            pipeline_mode=pl.Buffered(1))
    return pl.pallas_call(
        _ffn_body,
        grid=(rows // tm,),
        in_specs=[
            pl.BlockSpec((tm, D_MODEL), lambda i: (i, 0)),
            _const_spec((1, D_MODEL)),
            layer_spec((D_MODEL, 2 * D_FF)),
            layer_spec((D_FF, D_MODEL)),
        ],
        out_specs=pl.BlockSpec((tm, D_MODEL), lambda i: (i, 0)),
        out_shape=jax.ShapeDtypeStruct((rows, D_MODEL), F32),
        scratch_shapes=[pltpu.VMEM((tm, D_FF), BF16)],
        compiler_params=pltpu.CompilerParams(
            dimension_semantics=("arbitrary",), vmem_limit_bytes=VMEM_LIMIT),
        name="ffn",
    )(h2d, g.reshape(1, D_MODEL), wgu, wd)


def _s5_pitch(tt):
    return tt + 8


def _s5_body(h_ref, g_ref, win_ref, wb_ref, lam_ref, wc_ref, d_ref, wout_ref,
             x0r_ref, x0i_ref, o_ref, xfr_ref, xfi_ref, pit, u_tb, xs, y_tb, y_bt, st_r, st_i, *, nb, tt):
    rows = nb * tt
    pitch = _s5_pitch(tt)
    tiles = SSM_WIDTH // LANES

    @pl.when(pl.program_id(0) == 0)
    def _():
        st_r[...] = x0r_ref[...]
        st_i[...] = x0i_ref[...]

    h = h_ref[...].reshape(rows, D_MODEL)
    hn = _rms(h, g_ref[...]).astype(BF16)
    u = jnp.dot(hn, win_ref[...], preferred_element_type=F32)
    for b in range(nb):
        for j in range(tiles):
            pit[j, b * pitch:b * pitch + tt, :] = u[b * tt:(b + 1) * tt, j * LANES:(j + 1) * LANES]
    for t in range(tt):
        for j in range(tiles):
            u_tb[t * nb:(t + 1) * nb, j * LANES:(j + 1) * LANES] = pit[j, pl.ds(t, nb, stride=pitch), :]

    def expand(q):
        uq = u_tb[:, q * Q_IN:(q + 1) * Q_IN].astype(BF16)
        xs[q % 2] = jnp.dot(uq, wb_ref[q], preferred_element_type=F32)

    expand(0)
    for q in range(SSM_QUARTERS):
        if q + 1 < SSM_QUARTERS:
            expand(q + 1)
        buf = xs.at[q % 2]
        sl = slice(q * Q_STATE, (q + 1) * Q_STATE)
        ar = jnp.broadcast_to(lam_ref[0:1, sl], (nb, Q_STATE))
        ai = jnp.broadcast_to(lam_ref[1:2, sl], (nb, Q_STATE))
        xr, xi = st_r[:, sl], st_i[:, sl]
        for t in range(tt):
            r = slice(t * nb, (t + 1) * nb)
            nxr = ar * xr - ai * xi + buf[r, 0:Q_STATE]
            nxi = ar * xi + ai * xr + buf[r, Q_STATE:2 * Q_STATE]
            buf[r, 0:Q_STATE] = nxr
            buf[r, Q_STATE:2 * Q_STATE] = nxi
            xr, xi = nxr, nxi
        st_r[:, sl] = xr
        st_i[:, sl] = xi
        y_tb[:, q * Q_IN:(q + 1) * Q_IN] = jnp.dot(
            buf[...].astype(BF16), wc_ref[q], preferred_element_type=F32)

    d = d_ref[...]
    for t in range(tt):
        r = slice(t * nb, (t + 1) * nb)
        y = jax.nn.gelu(y_tb[r, :] + d * u_tb[r, :])
        for j in range(tiles):
            pit[j, pl.ds(t, nb, stride=pitch), :] = y[:, j * LANES:(j + 1) * LANES]
    for b in range(nb):
        for j in range(tiles):
            y_bt[b * tt:(b + 1) * tt, j * LANES:(j + 1) * LANES] = pit[j, b * pitch:b * pitch + tt, :].astype(BF16)
    z = jnp.dot(y_bt[...], wout_ref[...], preferred_element_type=F32)
    out = h + z[:, :D_MODEL] * jax.nn.sigmoid(z[:, D_MODEL:])
    o_ref[...] = out.reshape(nb, tt, D_MODEL)
    xfr_ref[...] = st_r[...]
    xfi_ref[...] = st_i[...]


def _s5_call(h, g, win, wb, lam, wc, d, wout, x0r, x0i, tt):
    nb, seq, _ = h.shape
    rows = nb * tt
    assert tt % 8 == 0 and nb % 8 == 0
    n_state = SSM_GROUPS * SSM_STATE
    body = functools.partial(_s5_body, nb=nb, tt=tt)
    return pl.pallas_call(
        body,
        grid=(seq // tt,),
        in_specs=[
            pl.BlockSpec((nb, tt, D_MODEL), lambda i: (0, i, 0)),
            _const_spec((1, D_MODEL)),
            _const_spec((D_MODEL, SSM_WIDTH)),
            _const_spec((SSM_QUARTERS, Q_IN, 2 * Q_STATE)),
            _const_spec((2, n_state)),
            _const_spec((SSM_QUARTERS, 2 * Q_STATE, Q_IN)),
            _const_spec((1, SSM_WIDTH)),
            _const_spec((SSM_WIDTH, 2 * D_MODEL)),
            _const_spec((nb, n_state)),
            _const_spec((nb, n_state)),
        ],
        out_specs=[
            pl.BlockSpec((nb, tt, D_MODEL), lambda i: (0, i, 0)),
            pl.BlockSpec((nb, n_state), lambda i: (0, 0)),
            pl.BlockSpec((nb, n_state), lambda i: (0, 0)),
        ],
        out_shape=[
            jax.ShapeDtypeStruct((nb, seq, D_MODEL), F32),
            jax.ShapeDtypeStruct((nb, n_state), F32),
            jax.ShapeDtypeStruct((nb, n_state), F32),
        ],
        scratch_shapes=[
            pltpu.VMEM((SSM_WIDTH // LANES, nb * _s5_pitch(tt), LANES), F32),
            pltpu.VMEM((rows, SSM_WIDTH), F32),
            pltpu.VMEM((2, rows, 2 * Q_STATE), F32),
            pltpu.VMEM((rows, SSM_WIDTH), F32),
            pltpu.VMEM((rows, SSM_WIDTH), BF16),
            pltpu.VMEM((nb, n_state), F32),
            pltpu.VMEM((nb, n_state), F32),
        ],
        compiler_params=pltpu.CompilerParams(
            dimension_semantics=("arbitrary",), vmem_limit_bytes=VMEM_LIMIT),
        name="s5_mixer",
    )(h, g.reshape(1, D_MODEL), win, wb, lam, wc, d.reshape(1, SSM_WIDTH), wout, x0r, x0i)


def _s5_discretise(lam_re, lam_im, b_re, b_im, c_re, c_im, log_step):
    lr, li = lam_re.astype(F32), lam_im.astype(F32)
    step = jnp.exp(log_step.astype(F32))[:, None]
    mag = jnp.exp(lr * step)
    ar = mag * jnp.cos(li * step)
    ai = mag * jnp.sin(li * step)
    den = lr * lr + li * li
    nr, ni = ar - 1.0, ai
    cr = (nr * lr + ni * li) / den
    ci = (ni * lr - nr * li) / den
    br, bi = b_re.astype(F32), b_im.astype(F32)
    bbar_r = cr[..., None] * br - ci[..., None] * bi
    bbar_i = cr[..., None] * bi + ci[..., None] * br
    eye = jnp.eye(Q_GROUPS, dtype=F32)

    def expand(bb):
        t = jnp.einsum('qgpc,gh->qgchp', bb.reshape(SSM_QUARTERS, Q_GROUPS, SSM_STATE, SSM_GROUP), eye)
        return t.reshape(SSM_QUARTERS, Q_IN, Q_STATE)

    def contract(cc):
        t = jnp.einsum('qgcp,gh->qgphc', cc.reshape(SSM_QUARTERS, Q_GROUPS, SSM_GROUP, SSM_STATE), eye)
        return t.reshape(SSM_QUARTERS, Q_STATE, Q_IN)

    wb = jnp.concatenate([expand(bbar_r), expand(bbar_i)], axis=-1).astype(BF16)
    wc = jnp.concatenate([contract(c_re.astype(F32)), -contract(c_im.astype(F32))], axis=1).astype(BF16)
    lam = jnp.stack([ar.reshape(-1), ai.reshape(-1)])
    return wb, lam, wc


def _head_norm_rope(x, gain, cos, sin_signed, seg):
    ms = jnp.dot((x * x).astype(BF16), seg, preferred_element_type=F32)
    xn = x * lax.rsqrt(ms + EPS) * gain
    lane = lax.broadcasted_iota(jnp.int32, xn.shape, 1)
    first_half = (lane % HEAD_DIM) < (HEAD_DIM // 2)
    partner = jnp.where(first_half,
                        pltpu.roll(xn, LANES - HEAD_DIM // 2, 1),
                        pltpu.roll(xn, HEAD_DIM // 2, 1))
    return xn * cos + partner * sin_signed


def _rope_tables(pos):
    half = HEAD_DIM // 2
    freqs = ROPE_THETA ** (-jnp.arange(0, half, dtype=F32) * 2.0 / HEAD_DIM)
    ang = pos.astype(F32)[:, None] * freqs[None, :]
    cos, sin = jnp.cos(ang), jnp.sin(ang)
    cos_t = jnp.tile(cos, (1, 2 * LANES // HEAD_DIM))
    sin_t = jnp.tile(jnp.concatenate([-sin, sin], axis=-1), (1, LANES // HEAD_DIM))
    return cos_t, sin_t


def _seg_matrix():
    i = jnp.arange(LANES)
    return ((i[:, None] // HEAD_DIM) == (i[None, :] // HEAD_DIM)).astype(F32).astype(BF16) * (1.0 / HEAD_DIM)


def _kv_body(h_ref, g_ref, w_ref, kg_ref, cos_ref, sin_ref, seg_ref, k_ref, v_ref, *, transpose_v):
    hn = _rms(h_ref[...], g_ref[...]).astype(BF16)
    kv = jnp.dot(hn, w_ref[...], preferred_element_type=F32)
    seg = seg_ref[...]
    for j in range(KV_WIDTH // LANES):
        kt = _head_norm_rope(kv[:, j * LANES:(j + 1) * LANES], kg_ref[...], cos_ref[...], sin_ref[...], seg)
        k_ref[:, j * LANES:(j + 1) * LANES] = kt.astype(BF16)
    if transpose_v:
        v_ref[0] = kv[:, KV_WIDTH:].T.astype(BF16)
    else:
        v_ref[...] = kv[:, KV_WIDTH:].astype(BF16)


def _kv_call(h2d, g, w_kv, kg128, cos_t, sin_t, seg, tm, seq_blocks, transpose_v):
    rows = h2d.shape[0]
    if transpose_v:
        v_spec = pl.BlockSpec((1, KV_WIDTH, tm), lambda i: (i // seq_blocks, 0, i % seq_blocks))
        v_shape = jax.ShapeDtypeStruct((rows // (tm * seq_blocks), KV_WIDTH, tm * seq_blocks), BF16)
    else:
        v_spec = pl.BlockSpec((tm, KV_WIDTH), lambda i: (i, 0))
        v_shape = jax.ShapeDtypeStruct((rows, KV_WIDTH), BF16)
    return pl.pallas_call(
        functools.partial(_kv_body, transpose_v=transpose_v),
        grid=(rows // tm,),
        in_specs=[
            pl.BlockSpec((tm, D_MODEL), lambda i: (i, 0)),
            _const_spec((1, D_MODEL)),
            _const_spec((D_MODEL, 2 * KV_WIDTH)),
            _const_spec((1, LANES)),
            pl.BlockSpec((tm, LANES), lambda i: (i % seq_blocks, 0)),
            pl.BlockSpec((tm, LANES), lambda i: (i % seq_blocks, 0)),
            _const_spec((LANES, LANES)),
        ],
        out_specs=[pl.BlockSpec((tm, KV_WIDTH), lambda i: (i, 0)), v_spec],
        out_shape=[jax.ShapeDtypeStruct((rows, KV_WIDTH), BF16), v_shape],
        compiler_params=pltpu.CompilerParams(
            dimension_semantics=("arbitrary",), vmem_limit_bytes=VMEM_LIMIT),
        name="shared_kv",
    )(h2d, g.reshape(1, D_MODEL), w_kv, kg128, cos_t, sin_t, seg)


HEAD_PAIRS = N_Q_HEADS // 2


def _attn_body(h_ref, g_ref, wq_ref, qg_ref, cos_ref, sin_ref, bias_ref, sink_ref,
               kp_ref, kc_ref, km_ref, vp_ref, vc_ref, vm_ref, wo_ref, o_ref, qp_scr, ot_scr):
    @pl.when((pl.program_id(0) == 0) & (pl.program_id(1) == 0))
    def _():
        qp_scr[...] = jnp.zeros_like(qp_scr)

    h = h_ref[0]
    hn = _rms(h, g_ref[...]).astype(BF16)
    qt = jnp.dot(hn, wq_ref[...], preferred_element_type=F32).T
    cos, sin, gain = cos_ref[...], sin_ref[...], qg_ref[...]
    half = HEAD_DIM // 2
    for hd in range(N_Q_HEADS):
        x = qt[hd * HEAD_DIM:(hd + 1) * HEAD_DIM, :]
        xn = x * lax.rsqrt(jnp.mean(x * x, axis=0, keepdims=True) + EPS) * gain
        x1, x2 = xn[:half], xn[half:]
        r = jnp.concatenate([x1 * cos - x2 * sin, x2 * cos + x1 * sin], axis=0).astype(BF16)
        row0 = ((hd // Q_PER_KV) % 2) * HEAD_DIM
        qp_scr[hd // 2, row0:row0 + HEAD_DIM, (hd % 2) * LANES:(hd % 2 + 1) * LANES] = r

    scores = []
    for p in range(HEAD_PAIRS):
        hk = p // (Q_PER_KV // 2)
        lane0 = (hk // 2) * LANES
        rhs = qp_scr[p]
        s_p = jnp.dot(kp_ref[0, :, lane0:lane0 + LANES], rhs, preferred_element_type=F32) + bias_ref[0, :BLOCK]
        s_c = jnp.dot(kc_ref[0, :, lane0:lane0 + LANES], rhs, preferred_element_type=F32) + bias_ref[0, BLOCK:]
        s_m = jnp.dot(km_ref[:, lane0:lane0 + LANES], rhs, preferred_element_type=F32)
        scores.append((s_p, s_c, s_m))

    for p in range(HEAD_PAIRS):
        hk = p // (Q_PER_KV // 2)
        s_p, s_c, s_m = scores[p]
        sink = sink_ref[p:p + 1, :]
        m = jnp.maximum(
            jnp.maximum(jnp.max(s_p, axis=0, keepdims=True), jnp.max(s_c, axis=0, keepdims=True)),
            jnp.maximum(jnp.max(s_m, axis=0, keepdims=True), sink))
        e_p = jnp.exp(s_p - m)
        e_c = jnp.exp(s_c - m)
        e_m = jnp.exp(s_m - m)
        den = (jnp.sum(e_p, axis=0, keepdims=True) + jnp.sum(e_c, axis=0, keepdims=True)
               + jnp.sum(e_m, axis=0, keepdims=True) + jnp.exp(sink - m))
        v0 = hk * HEAD_DIM
        ot = (jnp.dot(vp_ref[0, v0:v0 + HEAD_DIM, :], e_p.astype(BF16), preferred_element_type=F32)
              + jnp.dot(vc_ref[0, v0:v0 + HEAD_DIM, :], e_c.astype(BF16), preferred_element_type=F32)
              + jnp.dot(vm_ref[v0:v0 + HEAD_DIM, :], e_m.astype(BF16), preferred_element_type=F32))
        ot = ot * (1.0 / den)
        ot_scr[2 * p * HEAD_DIM:(2 * p + 1) * HEAD_DIM, :] = ot[:, :LANES]
        ot_scr[(2 * p + 1) * HEAD_DIM:(2 * p + 2) * HEAD_DIM, :] = ot[:, LANES:]
    o = ot_scr[...].T.astype(BF16)
    o_ref[0] = h + jnp.dot(o, wo_ref[...], preferred_element_type=F32)


def _attn_tables(q_gain, sinks, pos):
    half = HEAD_DIM // 2
    freqs = ROPE_THETA ** (-jnp.arange(0, half, dtype=F32) * 2.0 / HEAD_DIM)
    ang = pos.astype(F32)[:, None] * freqs[None, :]
    cos_t, sin_t = jnp.cos(ang).T, jnp.sin(ang).T
    gain = jnp.broadcast_to((q_gain.astype(F32) * HEAD_DIM ** -0.5)[:, None], (HEAD_DIM, LANES))
    kj = jnp.arange(BLOCK)[:, None]
    qi = jnp.arange(BLOCK)[None, :]
    neg = jnp.full((BLOCK, BLOCK), NEG_INF, F32)
    prev = jnp.where(kj > qi, 0.0, NEG_INF).astype(F32)
    cur = jnp.where(kj <= qi, 0.0, NEG_INF).astype(F32)
    bias = jnp.stack([jnp.concatenate([neg, cur]), jnp.concatenate([prev, cur])])
    bias = jnp.tile(bias, (1, 1, 2))
    sink_t = jnp.repeat(sinks.astype(F32).reshape(HEAD_PAIRS, 2), LANES, axis=1)
    return cos_t, sin_t, gain, bias, sink_t


def _attn_call(h, g, wq, tables, k, vt, k_meta, vt_meta, wo):
    nb, seq, _ = h.shape
    nblk = seq // BLOCK
    cos_t, sin_t, gain, bias, sink_t = tables
    prev = lambda n: jnp.maximum(n - 1, 0)
    half = HEAD_DIM // 2
    return pl.pallas_call(
        _attn_body,
        grid=(nb, nblk),
        in_specs=[
            pl.BlockSpec((1, BLOCK, D_MODEL), lambda b, n: (b, n, 0)),
            _const_spec((1, D_MODEL)),
            _const_spec((D_MODEL, D_MODEL)),
            _const_spec((HEAD_DIM, LANES)),
            pl.BlockSpec((half, BLOCK), lambda b, n: (0, n)),
            pl.BlockSpec((half, BLOCK), lambda b, n: (0, n)),
            pl.BlockSpec((1, 2 * BLOCK, 2 * LANES), lambda b, n: (jnp.minimum(n, 1), 0, 0)),
            _const_spec((HEAD_PAIRS, 2 * LANES)),
            pl.BlockSpec((1, BLOCK, KV_WIDTH), lambda b, n: (b, prev(n), 0)),
            pl.BlockSpec((1, BLOCK, KV_WIDTH), lambda b, n: (b, n, 0)),
            _const_spec((N_META, KV_WIDTH)),
            pl.BlockSpec((1, KV_WIDTH, BLOCK), lambda b, n: (b, 0, prev(n))),
            pl.BlockSpec((1, KV_WIDTH, BLOCK), lambda b, n: (b, 0, n)),
            _const_spec((KV_WIDTH, N_META)),
            _const_spec((D_MODEL, D_MODEL)),
        ],
        out_specs=pl.BlockSpec((1, BLOCK, D_MODEL), lambda b, n: (b, n, 0)),
        out_shape=jax.ShapeDtypeStruct((nb, seq, D_MODEL), F32),
        scratch_shapes=[pltpu.VMEM((HEAD_PAIRS, LANES, 2 * LANES), BF16),
                        pltpu.VMEM((D_MODEL, BLOCK), F32)],
        compiler_params=pltpu.CompilerParams(
            dimension_semantics=("arbitrary", "arbitrary"), vmem_limit_bytes=VMEM_LIMIT),
        name="swa_attention",
    )(h, g.reshape(1, D_MODEL), wq, gain, cos_t, sin_t, bias, sink_t, k, k, k_meta, vt, vt, vt_meta, wo)


def kernel(x, meta_tokens, ffn1_norm, ffn1_w_gate_up, ffn1_w_down, mix_norm, ffn2_norm, ffn2_w_gate_up,
           ffn2_w_down, ssm_w_in, ssm_lambda_re, ssm_lambda_im, ssm_b_re, ssm_b_im, ssm_c_re, ssm_c_im,
           ssm_log_step, ssm_d, ssm_w_out, kv_norm, w_kv, k_norm, attn_w_q, q_norm, attn_sinks, attn_w_o):
    nb, seq, _ = x.shape
    rows = nb * seq
    tm = 512
    tt = 64
    assert seq % tm == 0 and seq % tt == 0 and seq % BLOCK == 0
    bf = lambda w: w.astype(BF16)
    n_state = SSM_GROUPS * SSM_STATE

    wb, lam, wc = _s5_discretise(ssm_lambda_re[0], ssm_lambda_im[0], ssm_b_re[0], ssm_b_im[0],
                                 ssm_c_re[0], ssm_c_im[0], ssm_log_step[0])
    s5_w = (mix_norm[0], bf(ssm_w_in[0]))
    s5_w2 = (wb, lam, wc, ssm_d[0].astype(F32), bf(ssm_w_out[0]))
    ffn_w = ((bf(ffn1_w_gate_up), bf(ffn1_w_down)), (bf(ffn2_w_gate_up), bf(ffn2_w_down)))
    ffn = lambda h2d, which, layer, t: _ffn_call(
        h2d, (ffn1_norm, ffn2_norm)[which][layer], *ffn_w[which], layer, t)
    seg = _seg_matrix()
    w_kv_b = bf(w_kv)
    kg128 = jnp.tile(k_norm.astype(F32), LANES // HEAD_DIM).reshape(1, LANES)

    hm = ffn(meta_tokens.astype(F32), 0, 0, N_META)
    hm_b = jnp.broadcast_to(hm[None], (nb, N_META, D_MODEL))
    zeros = jnp.zeros((nb, n_state), F32)
    hm_b, x0r, x0i = _s5_call(hm_b, *s5_w, *s5_w2, zeros, zeros, N_META)
    hm = ffn(hm_b[0], 1, 0, N_META)
    cos_m, sin_m = _rope_tables(jnp.arange(N_META))
    k_meta, v_meta = _kv_call(hm, kv_norm, w_kv_b, kg128, cos_m, sin_m, seg, N_META, 1, False)

    h = ffn(x.reshape(rows, D_MODEL), 0, 0, tm)
    h, _, _ = _s5_call(h.reshape(nb, seq, D_MODEL), *s5_w, *s5_w2, x0r, x0i, tt)
    h = ffn(h.reshape(rows, D_MODEL), 1, 0, tm)

    cos_t, sin_t = _rope_tables(N_META + jnp.arange(seq))
    k, vt = _kv_call(h, kv_norm, w_kv_b, kg128, cos_t, sin_t, seg, tm, seq // tm, True)
    k = k.reshape(nb, seq, KV_WIDTH)

    h = ffn(h, 0, 1, tm)
    tables = _attn_tables(q_norm[0], attn_sinks[0], N_META + jnp.arange(seq))
    h = _attn_call(h.reshape(nb, seq, D_MODEL), mix_norm[1], bf(attn_w_q[0]), tables,
                   k, vt, k_meta, v_meta.T, bf(attn_w_o[0]))
    h = ffn(h.reshape(rows, D_MODEL), 1, 1, tm)
    return h.reshape(nb, seq, D_MODEL)
```

```python
import functools
import math

import jax
import jax.numpy as jnp
from jax import lax
from jax.experimental import pallas as pl
from jax.experimental.pallas import tpu as pltpu

F32 = jnp.float32
BF16 = jnp.bfloat16

D_MODEL = 1024
D_FF = 2816
N_META = 16
SSM_WIDTH = 512
SSM_GROUP = 16
SSM_GROUPS = 32
SSM_STATE = 64
HEAD_DIM = 64
N_Q_HEADS = 16
N_KV_HEADS = 4
Q_PER_KV = 4
KV_WIDTH = N_KV_HEADS * HEAD_DIM
BLOCK = 128
ROPE_THETA = 10000.0
EPS = 1e-6
NEG_INF = -1e30

LANES = 128
SSM_QUARTERS = 4
Q_GROUPS = SSM_GROUPS // SSM_QUARTERS
Q_IN = Q_GROUPS * SSM_GROUP
Q_STATE = Q_GROUPS * SSM_STATE
MXU_WIDTH = 256
FFN_CHUNK = MXU_WIDTH
FFN_SUB = 256
VMEM_LIMIT = 56 * 1024 * 1024


def _rms(x, g):
    return x * lax.rsqrt(jnp.mean(x * x, axis=-1, keepdims=True) + EPS) * g


def _const_spec(shape):
    nd = len(shape)
    return pl.BlockSpec(shape, lambda *_: (0,) * nd, pipeline_mode=pl.Buffered(1))


def _ffn_body(*refs, sub, with_kv):
    if with_kv:
        (x_ref, g_ref, wgu_ref, wd_ref, gkv_ref, wkv_ref, kg_ref, cos_ref, sin_ref, seg_ref,
         o_ref, k_ref, vt_ref, act_ref) = refs
    else:
        x_ref, g_ref, wgu_ref, wd_ref, o_ref, act_ref = refs
    for s in range(x_ref.shape[0] // sub):
        rs = slice(s * sub, (s + 1) * sub)
        x = x_ref[rs, :]
        xhat = x * lax.rsqrt(jnp.mean(x * x, axis=-1, keepdims=True) + EPS)
        xn = (xhat * g_ref[...]).astype(BF16)
        if with_kv:
            k, v = _kv_compute((xhat * gkv_ref[...]).astype(BF16), wkv_ref[...], kg_ref[...],
                               cos_ref[rs, :], sin_ref[rs, :], seg_ref[...])
            k_ref[rs, :] = k
            vt_ref[0, :, rs] = v.T.astype(BF16)
        for c in range(D_FF // FFN_CHUNK):
            lo = c * FFN_CHUNK
            a = jnp.dot(xn, wgu_ref[:, lo:lo + FFN_CHUNK], preferred_element_type=F32)
            b = jnp.dot(xn, wgu_ref[:, D_FF + lo:D_FF + lo + FFN_CHUNK], preferred_element_type=F32)
            act_ref[rs, lo:lo + FFN_CHUNK] = (a * jax.nn.sigmoid(a) * b).astype(BF16)
        y = jnp.dot(act_ref[rs, :], wd_ref[...], preferred_element_type=F32)
        o_ref[rs, :] = x + 0.5 * y


def _ffn_call(h2d, g, wgu, wd, layer, tm, kv=None):
    rows = h2d.shape[0]
    assert rows % tm == 0
    sub = min(tm, FFN_SUB)
    layer_spec = lambda shape: pl.BlockSpec((None,) + shape, lambda i: (layer, 0, 0),
                                            pipeline_mode=pl.Buffered(1))
    row_spec = pl.BlockSpec((tm, D_MODEL), lambda i: (i, 0))
    in_specs = [row_spec, _const_spec((1, D_MODEL)), layer_spec((D_MODEL, 2 * D_FF)), layer_spec((D_FF, D_MODEL))]
    args = [h2d, g.reshape(1, D_MODEL), wgu, wd]
    out_specs, out_shape = row_spec, jax.ShapeDtypeStruct((rows, D_MODEL), F32)
    if kv is not None:
        kv_norm, w_kv, kg128, cos_t, sin_t, seg, seq = kv
        assert seq % tm == 0
        sb = seq // tm
        table_spec = pl.BlockSpec((tm, LANES), lambda i: (i % sb, 0))
        in_specs += [_const_spec((1, D_MODEL)), _const_spec((D_MODEL, 2 * KV_WIDTH)), _const_spec((1, LANES)),
                     table_spec, table_spec, _const_spec((LANES, LANES))]
        args += [kv_norm.reshape(1, D_MODEL), w_kv, kg128, cos_t, sin_t, seg]
        out_specs = [row_spec, pl.BlockSpec((tm, KV_WIDTH), lambda i: (i, 0)),
                     pl.BlockSpec((1, KV_WIDTH, tm), lambda i: (i // sb, 0, i % sb))]
        out_shape = [out_shape, jax.ShapeDtypeStruct((rows, KV_WIDTH), BF16),
                     jax.ShapeDtypeStruct((rows // seq, KV_WIDTH, seq), BF16)]
    return pl.pallas_call(
        functools.partial(_ffn_body, sub=sub, with_kv=kv is not None),
        grid=(rows // tm,),
        in_specs=in_specs,
        out_specs=out_specs,
        out_shape=out_shape,
        scratch_shapes=[pltpu.VMEM((tm, D_FF), BF16)],
        compiler_params=pltpu.CompilerParams(
            dimension_semantics=("arbitrary",), vmem_limit_bytes=VMEM_LIMIT),
        name="ffn_kv" if kv is not None else "ffn",
    )(*args)


def _s5_pitch(tt):
    return tt + 8


def _s5_body(h_ref, g_ref, win_ref, wb_ref, lam_ref, wc_ref, d_ref, wout_ref,
             x0r_ref, x0i_ref, o_ref, xfr_ref, xfi_ref, pit, u_tb, xs, y_tb, y_bt, st_r, st_i, *, nb, tt):
    rows = nb * tt
    pitch = _s5_pitch(tt)
    tiles = SSM_WIDTH // LANES

    @pl.when(pl.program_id(0) == 0)
    def _():
        st_r[...] = x0r_ref[...]
        st_i[...] = x0i_ref[...]

    h = h_ref[...].reshape(rows, D_MODEL)
    hn = _rms(h, g_ref[...]).astype(BF16)
    u = jnp.dot(hn, win_ref[...], preferred_element_type=F32)
    for b in range(nb):
        for j in range(tiles):
            pit[j, b * pitch:b * pitch + tt, :] = u[b * tt:(b + 1) * tt, j * LANES:(j + 1) * LANES]
    for t in range(tt):
        for j in range(tiles):
            u_tb[t * nb:(t + 1) * nb, j * LANES:(j + 1) * LANES] = pit[j, pl.ds(t, nb, stride=pitch), :]

    def expand(q):
        uq = u_tb[:, q * Q_IN:(q + 1) * Q_IN].astype(BF16)
        xs[q % 2] = jnp.dot(uq, wb_ref[q], preferred_element_type=F32)

    expand(0)
    for q in range(SSM_QUARTERS):
        if q + 1 < SSM_QUARTERS:
            expand(q + 1)
        buf = xs.at[q % 2]
        sl = slice(q * Q_STATE, (q + 1) * Q_STATE)
        ar = jnp.broadcast_to(lam_ref[0:1, sl], (nb, Q_STATE))
        ai = jnp.broadcast_to(lam_ref[1:2, sl], (nb, Q_STATE))
        xr, xi = st_r[:, sl], st_i[:, sl]
        for t in range(tt):
            r = slice(t * nb, (t + 1) * nb)
            nxr = ar * xr - ai * xi + buf[r, 0:Q_STATE]
            nxi = ar * xi + ai * xr + buf[r, Q_STATE:2 * Q_STATE]
            buf[r, 0:Q_STATE] = nxr
            buf[r, Q_STATE:2 * Q_STATE] = nxi
            xr, xi = nxr, nxi
        st_r[:, sl] = xr
        st_i[:, sl] = xi
        y_tb[:, q * Q_IN:(q + 1) * Q_IN] = jnp.dot(
            buf[...].astype(BF16), wc_ref[q], preferred_element_type=F32)

    d = d_ref[...]
    for t in range(tt):
        r = slice(t * nb, (t + 1) * nb)
        y = jax.nn.gelu(y_tb[r, :] + d * u_tb[r, :])
        for j in range(tiles):
            pit[j, pl.ds(t, nb, stride=pitch), :] = y[:, j * LANES:(j + 1) * LANES]
    for b in range(nb):
        for j in range(tiles):
            y_bt[b * tt:(b + 1) * tt, j * LANES:(j + 1) * LANES] = pit[j, b * pitch:b * pitch + tt, :].astype(BF16)
    z = jnp.dot(y_bt[...], wout_ref[...], preferred_element_type=F32)
    out = h + z[:, :D_MODEL] * jax.nn.sigmoid(z[:, D_MODEL:])
    o_ref[...] = out.reshape(nb, tt, D_MODEL)
    xfr_ref[...] = st_r[...]
    xfi_ref[...] = st_i[...]


def _s5_call(h, g, win, wb, lam, wc, d, wout, x0r, x0i, tt):
    nb, seq, _ = h.shape
    rows = nb * tt
    assert tt % 8 == 0 and nb % 8 == 0
    n_state = SSM_GROUPS * SSM_STATE
    body = functools.partial(_s5_body, nb=nb, tt=tt)
    return pl.pallas_call(
        body,
        grid=(seq // tt,),
        in_specs=[
            pl.BlockSpec((nb, tt, D_MODEL), lambda i: (0, i, 0)),
            _const_spec((1, D_MODEL)),
            _const_spec((D_MODEL, SSM_WIDTH)),
            _const_spec((SSM_QUARTERS, Q_IN, 2 * Q_STATE)),
            _const_spec((2, n_state)),
            _const_spec((SSM_QUARTERS, 2 * Q_STATE, Q_IN)),
            _const_spec((1, SSM_WIDTH)),
            _const_spec((SSM_WIDTH, 2 * D_MODEL)),
            _const_spec((nb, n_state)),
            _const_spec((nb, n_state)),
        ],
        out_specs=[
            pl.BlockSpec((nb, tt, D_MODEL), lambda i: (0, i, 0)),
            pl.BlockSpec((nb, n_state), lambda i: (0, 0)),
            pl.BlockSpec((nb, n_state), lambda i: (0, 0)),
        ],
        out_shape=[
            jax.ShapeDtypeStruct((nb, seq, D_MODEL), F32),
            jax.ShapeDtypeStruct((nb, n_state), F32),
            jax.ShapeDtypeStruct((nb, n_state), F32),
        ],
        scratch_shapes=[
            pltpu.VMEM((SSM_WIDTH // LANES, nb * _s5_pitch(tt), LANES), F32),
            pltpu.VMEM((rows, SSM_WIDTH), F32),
            pltpu.VMEM((2, rows, 2 * Q_STATE), F32),
            pltpu.VMEM((rows, SSM_WIDTH), F32),
            pltpu.VMEM((rows, SSM_WIDTH), BF16),
            pltpu.VMEM((nb, n_state), F32),
            pltpu.VMEM((nb, n_state), F32),
        ],
        compiler_params=pltpu.CompilerParams(
            dimension_semantics=("arbitrary",), vmem_limit_bytes=VMEM_LIMIT),
        name="s5_mixer",
    )(h, g.reshape(1, D_MODEL), win, wb, lam, wc, d.reshape(1, SSM_WIDTH), wout, x0r, x0i)


def _s5_discretise(lam_re, lam_im, b_re, b_im, c_re, c_im, log_step):
    lr, li = lam_re.astype(F32), lam_im.astype(F32)
    step = jnp.exp(log_step.astype(F32))[:, None]
    mag = jnp.exp(lr * step)
    ar = mag * jnp.cos(li * step)
    ai = mag * jnp.sin(li * step)
    den = lr * lr + li * li
    nr, ni = ar - 1.0, ai
    cr = (nr * lr + ni * li) / den
    ci = (ni * lr - nr * li) / den
    br, bi = b_re.astype(F32), b_im.astype(F32)
    bbar_r = cr[..., None] * br - ci[..., None] * bi
    bbar_i = cr[..., None] * bi + ci[..., None] * br
    eye = jnp.eye(Q_GROUPS, dtype=F32)

    def expand(bb):
        t = jnp.einsum('qgpc,gh->qgchp', bb.reshape(SSM_QUARTERS, Q_GROUPS, SSM_STATE, SSM_GROUP), eye)
        return t.reshape(SSM_QUARTERS, Q_IN, Q_STATE)

    def contract(cc):
        t = jnp.einsum('qgcp,gh->qgphc', cc.reshape(SSM_QUARTERS, Q_GROUPS, SSM_GROUP, SSM_STATE), eye)
        return t.reshape(SSM_QUARTERS, Q_STATE, Q_IN)

    wb = jnp.concatenate([expand(bbar_r), expand(bbar_i)], axis=-1).astype(BF16)
    wc = jnp.concatenate([contract(c_re.astype(F32)), -contract(c_im.astype(F32))], axis=1).astype(BF16)
    lam = jnp.stack([ar.reshape(-1), ai.reshape(-1)])
    return wb, lam, wc


def _head_norm_rope(x, gain, cos, sin_signed, seg):
    ms = jnp.dot((x * x).astype(BF16), seg, preferred_element_type=F32)
    xn = x * lax.rsqrt(ms + EPS) * gain
    lane = lax.broadcasted_iota(jnp.int32, xn.shape, 1)
    first_half = (lane % HEAD_DIM) < (HEAD_DIM // 2)
    partner = jnp.where(first_half,
                        pltpu.roll(xn, LANES - HEAD_DIM // 2, 1),
                        pltpu.roll(xn, HEAD_DIM // 2, 1))
    return xn * cos + partner * sin_signed


def _rope_tables(pos):
    half = HEAD_DIM // 2
    freqs = ROPE_THETA ** (-jnp.arange(0, half, dtype=F32) * 2.0 / HEAD_DIM)
    ang = pos.astype(F32)[:, None] * freqs[None, :]
    cos, sin = jnp.cos(ang), jnp.sin(ang)
    cos_t = jnp.tile(cos, (1, 2 * LANES // HEAD_DIM))
    sin_t = jnp.tile(jnp.concatenate([-sin, sin], axis=-1), (1, LANES // HEAD_DIM))
    return cos_t, sin_t


def _seg_matrix():
    i = jnp.arange(LANES)
    return ((i[:, None] // HEAD_DIM) == (i[None, :] // HEAD_DIM)).astype(F32).astype(BF16) * (1.0 / HEAD_DIM)


def _kv_compute(hn, w_kv, k_gain, cos, sin_signed, seg):
    kv = jnp.dot(hn, w_kv, preferred_element_type=F32)
    k = jnp.concatenate(
        [_head_norm_rope(kv[:, j * LANES:(j + 1) * LANES], k_gain, cos, sin_signed, seg)
         for j in range(KV_WIDTH // LANES)], axis=1)
    return k.astype(BF16), kv[:, KV_WIDTH:]


def _kv_body(h_ref, g_ref, w_ref, kg_ref, cos_ref, sin_ref, seg_ref, k_ref, v_ref):
    hn = _rms(h_ref[...], g_ref[...]).astype(BF16)
    k, v = _kv_compute(hn, w_ref[...], kg_ref[...], cos_ref[...], sin_ref[...], seg_ref[...])
    k_ref[...] = k
    v_ref[...] = v.astype(BF16)


def _kv_call(h2d, g, w_kv, kg128, cos_t, sin_t, seg):
    rows = h2d.shape[0]
    return pl.pallas_call(
        _kv_body,
        grid=(1,),
        in_specs=[
            _const_spec((rows, D_MODEL)),
            _const_spec((1, D_MODEL)),
            _const_spec((D_MODEL, 2 * KV_WIDTH)),
            _const_spec((1, LANES)),
            _const_spec((rows, LANES)),
            _const_spec((rows, LANES)),
            _const_spec((LANES, LANES)),
        ],
        out_specs=[pl.BlockSpec((rows, KV_WIDTH), lambda i: (0, 0)),
                   pl.BlockSpec((rows, KV_WIDTH), lambda i: (0, 0))],
        out_shape=[jax.ShapeDtypeStruct((rows, KV_WIDTH), BF16),
                   jax.ShapeDtypeStruct((rows, KV_WIDTH), BF16)],
        compiler_params=pltpu.CompilerParams(
            dimension_semantics=("arbitrary",), vmem_limit_bytes=VMEM_LIMIT),
        name="shared_kv",
    )(h2d, g.reshape(1, D_MODEL), w_kv, kg128, cos_t, sin_t, seg)


HEAD_PAIRS = N_Q_HEADS // 2
ATTN_SUB = 4


def _attn_body(h_ref, g_ref, wq_ref, qg_ref, cos_ref, sin_ref, bias_ref, sink_ref,
               kp_ref, kc_ref, km_ref, vp_ref, vc_ref, vm_ref, wo_ref, o_ref, qp_scr, ot_scr):
    n = pl.program_id(1)

    @pl.when((pl.program_id(0) == 0) & (n == 0))
    def _():
        qp_scr[...] = jnp.zeros_like(qp_scr)

    gain = qg_ref[...]
    half = HEAD_DIM // 2

    def scores_of(s):
        rows = slice(s * BLOCK, (s + 1) * BLOCK)
        hn = _rms(h_ref[0, rows, :], g_ref[...]).astype(BF16)
        qt = jnp.dot(hn, wq_ref[...], preferred_element_type=F32).T
        cos, sin = cos_ref[:, rows], sin_ref[:, rows]
        for hd in range(N_Q_HEADS):
            x = qt[hd * HEAD_DIM:(hd + 1) * HEAD_DIM, :]
            xn = x * lax.rsqrt(jnp.mean(x * x, axis=0, keepdims=True) + EPS) * gain
            x1, x2 = xn[:half], xn[half:]
            r = jnp.concatenate([x1 * cos - x2 * sin, x2 * cos + x1 * sin], axis=0).astype(BF16)
            row0 = ((hd // Q_PER_KV) % 2) * HEAD_DIM
            qp_scr[s, hd // 2, row0:row0 + HEAD_DIM, (hd % 2) * LANES:(hd % 2 + 1) * LANES] = r
        if s == 0:
            k_prev = lambda l0: kp_ref[0, :, l0:l0 + LANES]
            bias = bias_ref[jnp.minimum(n, 1)]
        else:
            k_prev = lambda l0: kc_ref[0, (s - 1) * BLOCK:s * BLOCK, l0:l0 + LANES]
            bias = bias_ref[1]
        out = []
        for p in range(HEAD_PAIRS):
            hk = p // (Q_PER_KV // 2)
            lane0 = (hk // 2) * LANES
            rhs = qp_scr[s, p]
            s_p = jnp.dot(k_prev(lane0), rhs, preferred_element_type=F32) + bias[:BLOCK]
            s_c = jnp.dot(kc_ref[0, rows, lane0:lane0 + LANES], rhs, preferred_element_type=F32) + bias[BLOCK:]
            s_m = jnp.dot(km_ref[:, lane0:lane0 + LANES], rhs, preferred_element_type=F32)
            out.append((s_p, s_c, s_m))
        return out

    def finish(s, scores):
        rows = slice(s * BLOCK, (s + 1) * BLOCK)
        for p in range(HEAD_PAIRS):
            hk = p // (Q_PER_KV // 2)
            s_p, s_c, s_m = scores[p]
            sink = sink_ref[p:p + 1, :]
            m = jnp.maximum(
                jnp.maximum(jnp.max(s_p, axis=0, keepdims=True), jnp.max(s_c, axis=0, keepdims=True)),
                jnp.maximum(jnp.max(s_m, axis=0, keepdims=True), sink))
            e_p = jnp.exp(s_p - m)
            e_c = jnp.exp(s_c - m)
            e_m = jnp.exp(s_m - m)
            den = (jnp.sum(e_p, axis=0, keepdims=True) + jnp.sum(e_c, axis=0, keepdims=True)
                   + jnp.sum(e_m, axis=0, keepdims=True) + jnp.exp(sink - m))
            v0 = hk * HEAD_DIM
            if s == 0:
                v_prev = vp_ref[0, v0:v0 + HEAD_DIM, :]
            else:
                v_prev = vc_ref[0, v0:v0 + HEAD_DIM, (s - 1) * BLOCK:s * BLOCK]
            ot = (jnp.dot(v_prev, e_p.astype(BF16), preferred_element_type=F32)
                  + jnp.dot(vc_ref[0, v0:v0 + HEAD_DIM, rows], e_c.astype(BF16), preferred_element_type=F32)
                  + jnp.dot(vm_ref[v0:v0 + HEAD_DIM, :], e_m.astype(BF16), preferred_element_type=F32))
            ot = ot * (1.0 / den)
            ot_scr[s, 2 * p * HEAD_DIM:(2 * p + 1) * HEAD_DIM, :] = ot[:, :LANES]
            ot_scr[s, (2 * p + 1) * HEAD_DIM:(2 * p + 2) * HEAD_DIM, :] = ot[:, LANES:]
        o = ot_scr[s].T.astype(BF16)
        o_ref[0, rows, :] = h_ref[0, rows, :] + jnp.dot(o, wo_ref[...], preferred_element_type=F32)

    scores = [scores_of(s) for s in range(ATTN_SUB)]
    for s in range(ATTN_SUB):
        finish(s, scores[s])


def _attn_tables(q_gain, sinks, pos):
    half = HEAD_DIM // 2
    freqs = ROPE_THETA ** (-jnp.arange(0, half, dtype=F32) * 2.0 / HEAD_DIM)
    ang = pos.astype(F32)[:, None] * freqs[None, :]
    cos_t, sin_t = jnp.cos(ang).T, jnp.sin(ang).T
    gain = jnp.broadcast_to((q_gain.astype(F32) * HEAD_DIM ** -0.5)[:, None], (HEAD_DIM, LANES))
    kj = jnp.arange(BLOCK)[:, None]
    qi = jnp.arange(BLOCK)[None, :]
    neg = jnp.full((BLOCK, BLOCK), NEG_INF, F32)
    prev = jnp.where(kj > qi, 0.0, NEG_INF).astype(F32)
    cur = jnp.where(kj <= qi, 0.0, NEG_INF).astype(F32)
    bias = jnp.stack([jnp.concatenate([neg, cur]), jnp.concatenate([prev, cur])])
    bias = jnp.tile(bias, (1, 1, 2))
    sink_t = jnp.repeat(sinks.astype(F32).reshape(HEAD_PAIRS, 2), LANES, axis=1)
    return cos_t, sin_t, gain, bias, sink_t


def _attn_call(h, g, wq, tables, k, vt, k_meta, vt_meta, wo):
    nb, seq, _ = h.shape
    qb = ATTN_SUB * BLOCK
    assert seq % qb == 0
    cos_t, sin_t, gain, bias, sink_t = tables
    prev = lambda n: jnp.maximum(ATTN_SUB * n - 1, 0)
    half = HEAD_DIM // 2
    return pl.pallas_call(
        _attn_body,
        grid=(nb, seq // qb),
        in_specs=[
            pl.BlockSpec((1, qb, D_MODEL), lambda b, n: (b, n, 0)),
            _const_spec((1, D_MODEL)),
            _const_spec((D_MODEL, D_MODEL)),
            _const_spec((HEAD_DIM, LANES)),
            pl.BlockSpec((half, qb), lambda b, n: (0, n)),
            pl.BlockSpec((half, qb), lambda b, n: (0, n)),
            _const_spec((2, 2 * BLOCK, 2 * LANES)),
            _const_spec((HEAD_PAIRS, 2 * LANES)),
            pl.BlockSpec((1, BLOCK, KV_WIDTH), lambda b, n: (b, prev(n), 0)),
            pl.BlockSpec((1, qb, KV_WIDTH), lambda b, n: (b, n, 0)),
            _const_spec((N_META, KV_WIDTH)),
            pl.BlockSpec((1, KV_WIDTH, BLOCK), lambda b, n: (b, 0, prev(n))),
            pl.BlockSpec((1, KV_WIDTH, qb), lambda b, n: (b, 0, n)),
            _const_spec((KV_WIDTH, N_META)),
            _const_spec((D_MODEL, D_MODEL)),
        ],
        out_specs=pl.BlockSpec((1, qb, D_MODEL), lambda b, n: (b, n, 0)),
        out_shape=jax.ShapeDtypeStruct((nb, seq, D_MODEL), F32),
        scratch_shapes=[pltpu.VMEM((ATTN_SUB, HEAD_PAIRS, LANES, 2 * LANES), BF16),
                        pltpu.VMEM((ATTN_SUB, D_MODEL, BLOCK), F32)],
        compiler_params=pltpu.CompilerParams(
            dimension_semantics=("arbitrary", "arbitrary"), vmem_limit_bytes=VMEM_LIMIT),
        name="swa_attention",
    )(h, g.reshape(1, D_MODEL), wq, gain, cos_t, sin_t, bias, sink_t, k, k, k_meta, vt, vt, vt_meta, wo)


def kernel(x, meta_tokens, ffn1_norm, ffn1_w_gate_up, ffn1_w_down, mix_norm, ffn2_norm, ffn2_w_gate_up,
           ffn2_w_down, ssm_w_in, ssm_lambda_re, ssm_lambda_im, ssm_b_re, ssm_b_im, ssm_c_re, ssm_c_im,
           ssm_log_step, ssm_d, ssm_w_out, kv_norm, w_kv, k_norm, attn_w_q, q_norm, attn_sinks, attn_w_o):
    nb, seq, _ = x.shape
    rows = nb * seq
    tm = 1024
    tt = 64
    assert seq % tm == 0 and seq % tt == 0
    bf = lambda w: w.astype(BF16)
    n_state = SSM_GROUPS * SSM_STATE

    wb, lam, wc = _s5_discretise(ssm_lambda_re[0], ssm_lambda_im[0], ssm_b_re[0], ssm_b_im[0],
                                 ssm_c_re[0], ssm_c_im[0], ssm_log_step[0])
    s5_w = (mix_norm[0], bf(ssm_w_in[0]))
    s5_w2 = (wb, lam, wc, ssm_d[0].astype(F32), bf(ssm_w_out[0]))
    ffn_w = ((bf(ffn1_w_gate_up), bf(ffn1_w_down)), (bf(ffn2_w_gate_up), bf(ffn2_w_down)))
    ffn = lambda h2d, which, layer, t: _ffn_call(
        h2d, (ffn1_norm, ffn2_norm)[which][layer], *ffn_w[which], layer, t)
    seg = _seg_matrix()
    w_kv_b = bf(w_kv)
    kg128 = jnp.tile(k_norm.astype(F32), LANES // HEAD_DIM).reshape(1, LANES)

    hm = ffn(meta_tokens.astype(F32), 0, 0, N_META)
    hm_b = jnp.broadcast_to(hm[None], (nb, N_META, D_MODEL))
    zeros = jnp.zeros((nb, n_state), F32)
    hm_b, x0r, x0i = _s5_call(hm_b, *s5_w, *s5_w2, zeros, zeros, N_META)
    hm = ffn(hm_b[0], 1, 0, N_META)
    cos_m, sin_m = _rope_tables(jnp.arange(N_META))
    k_meta, v_meta = _kv_call(hm, kv_norm, w_kv_b, kg128, cos_m, sin_m, seg)

    h = ffn(x.reshape(rows, D_MODEL), 0, 0, tm)
    h, _, _ = _s5_call(h.reshape(nb, seq, D_MODEL), *s5_w, *s5_w2, x0r, x0i, tt)
    h = ffn(h.reshape(rows, D_MODEL), 1, 0, tm)

    cos_t, sin_t = _rope_tables(N_META + jnp.arange(seq))
    h, k, vt = _ffn_call(h, ffn1_norm[1], *ffn_w[0], 1, tm,
                         kv=(kv_norm, w_kv_b, kg128, cos_t, sin_t, seg, seq))
    k = k.reshape(nb, seq, KV_WIDTH)
    tables = _attn_tables(q_norm[0], attn_sinks[0], N_META + jnp.arange(seq))
    h = _attn_call(h.reshape(nb, seq, D_MODEL), mix_norm[1], bf(attn_w_q[0]), tables,
                   k, vt, k_meta, v_meta.T, bf(attn_w_o[0]))
    h = ffn(h.reshape(rows, D_MODEL), 1, 1, tm)
    return h.reshape(nb, seq, D_MODEL)
```

```python
import functools
import math

import jax
import jax.numpy as jnp
from jax import lax
from jax.experimental import pallas as pl
from jax.experimental.pallas import tpu as pltpu

F32 = jnp.float32
BF16 = jnp.bfloat16

D_MODEL = 1024
D_FF = 2816
N_META = 16
SSM_WIDTH = 512
SSM_GROUP = 16
SSM_GROUPS = 32
SSM_STATE = 64
HEAD_DIM = 64
N_Q_HEADS = 16
N_KV_HEADS = 4
Q_PER_KV = 4
KV_WIDTH = N_KV_HEADS * HEAD_DIM
BLOCK = 128
ROPE_THETA = 10000.0
EPS = 1e-6
NEG_INF = -1e30

LANES = 128
SSM_QUARTERS = 4
Q_GROUPS = SSM_GROUPS // SSM_QUARTERS
Q_IN = Q_GROUPS * SSM_GROUP
Q_STATE = Q_GROUPS * SSM_STATE
MXU_WIDTH = 256
FFN_CHUNK = MXU_WIDTH
FFN_SUB = 256
VMEM_LIMIT = 56 * 1024 * 1024


def _rms(x, g):
    return x * lax.rsqrt(jnp.mean(x * x, axis=-1, keepdims=True) + EPS) * g


def _const_spec(shape):
    nd = len(shape)
    return pl.BlockSpec(shape, lambda *_: (0,) * nd, pipeline_mode=pl.Buffered(1))


def _ffn_body(*refs, sub, with_kv):
    if with_kv:
        (x_ref, g_ref, wgu_ref, wd_ref, gkv_ref, wkv_ref, kg_ref, cos_ref, sin_ref, seg_ref,
         o_ref, k_ref, vt_ref, act_ref) = refs
    else:
        x_ref, g_ref, wgu_ref, wd_ref, o_ref, act_ref = refs
    for s in range(x_ref.shape[0] // sub):
        rs = slice(s * sub, (s + 1) * sub)
        x = x_ref[rs, :]
        xhat = x * lax.rsqrt(jnp.mean(x * x, axis=-1, keepdims=True) + EPS)
        xn = (xhat * g_ref[...]).astype(BF16)
        if with_kv:
            k, v = _kv_compute((xhat * gkv_ref[...]).astype(BF16), wkv_ref[...], kg_ref[...],
                               cos_ref[rs, :], sin_ref[rs, :], seg_ref[...])
            k_ref[rs, :] = k
            vt_ref[0, :, rs] = v.T.astype(BF16)
        for c in range(D_FF // FFN_CHUNK):
            lo = c * FFN_CHUNK
            a = jnp.dot(xn, wgu_ref[:, lo:lo + FFN_CHUNK], preferred_element_type=F32)
            b = jnp.dot(xn, wgu_ref[:, D_FF + lo:D_FF + lo + FFN_CHUNK], preferred_element_type=F32)
            act_ref[rs, lo:lo + FFN_CHUNK] = (a * jax.nn.sigmoid(a) * b).astype(BF16)
        y = jnp.dot(act_ref[rs, :], wd_ref[...], preferred_element_type=F32)
        o_ref[rs, :] = x + 0.5 * y


def _ffn_call(h2d, g, wgu, wd, layer, tm, kv=None):
    rows = h2d.shape[0]
    assert rows % tm == 0
    sub = min(tm, FFN_SUB)
    layer_spec = lambda shape: pl.BlockSpec((None,) + shape, lambda i: (layer, 0, 0),
                                            pipeline_mode=pl.Buffered(1))
    row_spec = pl.BlockSpec((tm, D_MODEL), lambda i: (i, 0))
    in_specs = [row_spec, _const_spec((1, D_MODEL)), layer_spec((D_MODEL, 2 * D_FF)), layer_spec((D_FF, D_MODEL))]
    args = [h2d, g.reshape(1, D_MODEL), wgu, wd]
    out_specs, out_shape = row_spec, jax.ShapeDtypeStruct((rows, D_MODEL), F32)
    if kv is not None:
        kv_norm, w_kv, kg128, cos_t, sin_t, seg, seq = kv
        assert seq % tm == 0
        sb = seq // tm
        table_spec = pl.BlockSpec((tm, LANES), lambda i: (i % sb, 0))
        in_specs += [_const_spec((1, D_MODEL)), _const_spec((D_MODEL, 2 * KV_WIDTH)), _const_spec((1, LANES)),
                     table_spec, table_spec, _const_spec((LANES, LANES))]
        args += [kv_norm.reshape(1, D_MODEL), w_kv, kg128, cos_t, sin_t, seg]
        out_specs = [row_spec, pl.BlockSpec((tm, KV_WIDTH), lambda i: (i, 0)),
                     pl.BlockSpec((1, KV_WIDTH, tm), lambda i: (i // sb, 0, i % sb))]
        out_shape = [out_shape, jax.ShapeDtypeStruct((rows, KV_WIDTH), BF16),
                     jax.ShapeDtypeStruct((rows // seq, KV_WIDTH, seq), BF16)]
    return pl.pallas_call(
        functools.partial(_ffn_body, sub=sub, with_kv=kv is not None),
        grid=(rows // tm,),
        in_specs=in_specs,
        out_specs=out_specs,
        out_shape=out_shape,
        scratch_shapes=[pltpu.VMEM((tm, D_FF), BF16)],
        compiler_params=pltpu.CompilerParams(
            dimension_semantics=("arbitrary",), vmem_limit_bytes=VMEM_LIMIT),
        name="ffn_kv" if kv is not None else "ffn",
    )(*args)


def _s5_pitch(tt):
    return tt + 8


def _s5_body(h_ref, g_ref, win_ref, wb_ref, lam_ref, wc_ref, d_ref, wout_ref,
             x0r_ref, x0i_ref, o_ref, xfr_ref, xfi_ref, pit, u_tb, xs, y_tb, y_bt, st_r, st_i, *, nb, tt):
    rows = nb * tt
    pitch = _s5_pitch(tt)
    tiles = SSM_WIDTH // LANES

    @pl.when(pl.program_id(0) == 0)
    def _():
        st_r[...] = x0r_ref[...]
        st_i[...] = x0i_ref[...]

    h = h_ref[...].reshape(rows, D_MODEL)
    hn = _rms(h, g_ref[...]).astype(BF16)
    u = jnp.dot(hn, win_ref[...], preferred_element_type=F32)
    for b in range(nb):
        for j in range(tiles):
            pit[j, b * pitch:b * pitch + tt, :] = u[b * tt:(b + 1) * tt, j * LANES:(j + 1) * LANES]
    for t in range(tt):
        for j in range(tiles):
            u_tb[t * nb:(t + 1) * nb, j * LANES:(j + 1) * LANES] = pit[j, pl.ds(t, nb, stride=pitch), :]

    def expand(q):
        uq = u_tb[:, q * Q_IN:(q + 1) * Q_IN].astype(BF16)
        xs[q % 2] = jnp.dot(uq, wb_ref[q], preferred_element_type=F32)

    expand(0)
    for q in range(SSM_QUARTERS):
        if q + 1 < SSM_QUARTERS:
            expand(q + 1)
        buf = xs.at[q % 2]
        sl = slice(q * Q_STATE, (q + 1) * Q_STATE)
        ar = jnp.broadcast_to(lam_ref[0:1, sl], (nb, Q_STATE))
        ai = jnp.broadcast_to(lam_ref[1:2, sl], (nb, Q_STATE))
        xr, xi = st_r[:, sl], st_i[:, sl]
        for t in range(tt):
            r = slice(t * nb, (t + 1) * nb)
            nxr = ar * xr - ai * xi + buf[r, 0:Q_STATE]
            nxi = ar * xi + ai * xr + buf[r, Q_STATE:2 * Q_STATE]
            buf[r, 0:Q_STATE] = nxr
            buf[r, Q_STATE:2 * Q_STATE] = nxi
            xr, xi = nxr, nxi
        st_r[:, sl] = xr
        st_i[:, sl] = xi
        y_tb[:, q * Q_IN:(q + 1) * Q_IN] = jnp.dot(
            buf[...].astype(BF16), wc_ref[q], preferred_element_type=F32)

    d = d_ref[...]
    for t in range(tt):
        r = slice(t * nb, (t + 1) * nb)
        y = jax.nn.gelu(y_tb[r, :] + d * u_tb[r, :])
        for j in range(tiles):
            pit[j, pl.ds(t, nb, stride=pitch), :] = y[:, j * LANES:(j + 1) * LANES]
    for b in range(nb):
        for j in range(tiles):
            y_bt[b * tt:(b + 1) * tt, j * LANES:(j + 1) * LANES] = pit[j, b * pitch:b * pitch + tt, :].astype(BF16)
    z = jnp.dot(y_bt[...], wout_ref[...], preferred_element_type=F32)
    out = h + z[:, :D_MODEL] * jax.nn.sigmoid(z[:, D_MODEL:])
    o_ref[...] = out.reshape(nb, tt, D_MODEL)
    xfr_ref[...] = st_r[...]
    xfi_ref[...] = st_i[...]


def _s5_call(h, g, win, wb, lam, wc, d, wout, x0r, x0i, tt):
    nb, seq, _ = h.shape
    rows = nb * tt
    assert tt % 8 == 0 and nb % 8 == 0
    n_state = SSM_GROUPS * SSM_STATE
    body = functools.partial(_s5_body, nb=nb, tt=tt)
    return pl.pallas_call(
        body,
        grid=(seq // tt,),
        in_specs=[
            pl.BlockSpec((nb, tt, D_MODEL), lambda i: (0, i, 0)),
            _const_spec((1, D_MODEL)),
            _const_spec((D_MODEL, SSM_WIDTH)),
            _const_spec((SSM_QUARTERS, Q_IN, 2 * Q_STATE)),
            _const_spec((2, n_state)),
            _const_spec((SSM_QUARTERS, 2 * Q_STATE, Q_IN)),
            _const_spec((1, SSM_WIDTH)),
            _const_spec((SSM_WIDTH, 2 * D_MODEL)),
            _const_spec((nb, n_state)),
            _const_spec((nb, n_state)),
        ],
        out_specs=[
            pl.BlockSpec((nb, tt, D_MODEL), lambda i: (0, i, 0)),
            pl.BlockSpec((nb, n_state), lambda i: (0, 0)),
            pl.BlockSpec((nb, n_state), lambda i: (0, 0)),
        ],
        out_shape=[
            jax.ShapeDtypeStruct((nb, seq, D_MODEL), F32),
            jax.ShapeDtypeStruct((nb, n_state), F32),
            jax.ShapeDtypeStruct((nb, n_state), F32),
        ],
        scratch_shapes=[
            pltpu.VMEM((SSM_WIDTH // LANES, nb * _s5_pitch(tt), LANES), F32),
            pltpu.VMEM((rows, SSM_WIDTH), F32),
            pltpu.VMEM((2, rows, 2 * Q_STATE), F32),
            pltpu.VMEM((rows, SSM_WIDTH), F32),
            pltpu.VMEM((rows, SSM_WIDTH), BF16),
            pltpu.VMEM((nb, n_state), F32),
            pltpu.VMEM((nb, n_state), F32),
        ],
        compiler_params=pltpu.CompilerParams(
            dimension_semantics=("arbitrary",), vmem_limit_bytes=VMEM_LIMIT),
        name="s5_mixer",
    )(h, g.reshape(1, D_MODEL), win, wb, lam, wc, d.reshape(1, SSM_WIDTH), wout, x0r, x0i)


def _s5_discretise(lam_re, lam_im, b_re, b_im, c_re, c_im, log_step):
    lr, li = lam_re.astype(F32), lam_im.astype(F32)
    step = jnp.exp(log_step.astype(F32))[:, None]
    mag = jnp.exp(lr * step)
    ar = mag * jnp.cos(li * step)
    ai = mag * jnp.sin(li * step)
    den = lr * lr + li * li
    nr, ni = ar - 1.0, ai
    cr = (nr * lr + ni * li) / den
    ci = (ni * lr - nr * li) / den
    br, bi = b_re.astype(F32), b_im.astype(F32)
    bbar_r = cr[..., None] * br - ci[..., None] * bi
    bbar_i = cr[..., None] * bi + ci[..., None] * br
    eye = jnp.eye(Q_GROUPS, dtype=F32)

    def expand(bb):
        t = jnp.einsum('qgpc,gh->qgchp', bb.reshape(SSM_QUARTERS, Q_GROUPS, SSM_STATE, SSM_GROUP), eye)
        return t.reshape(SSM_QUARTERS, Q_IN, Q_STATE)

    def contract(cc):
        t = jnp.einsum('qgcp,gh->qgphc', cc.reshape(SSM_QUARTERS, Q_GROUPS, SSM_GROUP, SSM_STATE), eye)
        return t.reshape(SSM_QUARTERS, Q_STATE, Q_IN)

    wb = jnp.concatenate([expand(bbar_r), expand(bbar_i)], axis=-1).astype(BF16)
    wc = jnp.concatenate([contract(c_re.astype(F32)), -contract(c_im.astype(F32))], axis=1).astype(BF16)
    lam = jnp.stack([ar.reshape(-1), ai.reshape(-1)])
    return wb, lam, wc


def _head_norm_rope(x, gain, cos, sin_signed, seg):
    ms = jnp.dot((x * x).astype(BF16), seg, preferred_element_type=F32)
    xn = x * lax.rsqrt(ms + EPS) * gain
    lane = lax.broadcasted_iota(jnp.int32, xn.shape, 1)
    first_half = (lane % HEAD_DIM) < (HEAD_DIM // 2)
    partner = jnp.where(first_half,
                        pltpu.roll(xn, LANES - HEAD_DIM // 2, 1),
                        pltpu.roll(xn, HEAD_DIM // 2, 1))
    return xn * cos + partner * sin_signed


def _rope_tables(pos):
    half = HEAD_DIM // 2
    freqs = ROPE_THETA ** (-jnp.arange(0, half, dtype=F32) * 2.0 / HEAD_DIM)
    ang = pos.astype(F32)[:, None] * freqs[None, :]
    cos, sin = jnp.cos(ang), jnp.sin(ang)
    cos_t = jnp.tile(cos, (1, 2 * LANES // HEAD_DIM))
    sin_t = jnp.tile(jnp.concatenate([-sin, sin], axis=-1), (1, LANES // HEAD_DIM))
    return cos_t, sin_t


def _seg_matrix():
    i = jnp.arange(LANES)
    return ((i[:, None] // HEAD_DIM) == (i[None, :] // HEAD_DIM)).astype(F32).astype(BF16) * (1.0 / HEAD_DIM)


def _kv_compute(hn, w_kv, k_gain, cos, sin_signed, seg):
    kv = jnp.dot(hn, w_kv, preferred_element_type=F32)
    k = jnp.concatenate(
        [_head_norm_rope(kv[:, j * LANES:(j + 1) * LANES], k_gain, cos, sin_signed, seg)
         for j in range(KV_WIDTH // LANES)], axis=1)
    return k.astype(BF16), kv[:, KV_WIDTH:]


def _kv_body(h_ref, g_ref, w_ref, kg_ref, cos_ref, sin_ref, seg_ref, k_ref, v_ref):
    hn = _rms(h_ref[...], g_ref[...]).astype(BF16)
    k, v = _kv_compute(hn, w_ref[...], kg_ref[...], cos_ref[...], sin_ref[...], seg_ref[...])
    k_ref[...] = k
    v_ref[...] = v.astype(BF16)


def _kv_call(h2d, g, w_kv, kg128, cos_t, sin_t, seg):
    rows = h2d.shape[0]
    return pl.pallas_call(
        _kv_body,
        grid=(1,),
        in_specs=[
            _const_spec((rows, D_MODEL)),
            _const_spec((1, D_MODEL)),
            _const_spec((D_MODEL, 2 * KV_WIDTH)),
            _const_spec((1, LANES)),
            _const_spec((rows, LANES)),
            _const_spec((rows, LANES)),
            _const_spec((LANES, LANES)),
        ],
        out_specs=[pl.BlockSpec((rows, KV_WIDTH), lambda i: (0, 0)),
                   pl.BlockSpec((rows, KV_WIDTH), lambda i: (0, 0))],
        out_shape=[jax.ShapeDtypeStruct((rows, KV_WIDTH), BF16),
                   jax.ShapeDtypeStruct((rows, KV_WIDTH), BF16)],
        compiler_params=pltpu.CompilerParams(
            dimension_semantics=("arbitrary",), vmem_limit_bytes=VMEM_LIMIT),
        name="shared_kv",
    )(h2d, g.reshape(1, D_MODEL), w_kv, kg128, cos_t, sin_t, seg)


HEAD_PAIRS = N_Q_HEADS // 2
ATTN_SUB = 4


def _attn_body(h_ref, g_ref, wq_ref, qg_ref, cos_ref, sin_ref, bias_ref, sink_ref,
               kp_ref, kc_ref, km_ref, vp_ref, vc_ref, vm_ref, wo_ref, o_ref, qp_scr, ot_scr):
    n = pl.program_id(1)

    @pl.when((pl.program_id(0) == 0) & (n == 0))
    def _():
        qp_scr[...] = jnp.zeros_like(qp_scr)

    gain = qg_ref[...]
    half = HEAD_DIM // 2
    hn = _rms(h_ref[0], g_ref[...]).astype(BF16)
    q_all = jnp.dot(hn, wq_ref[...], preferred_element_type=F32)

    def scores_of(s):
        rows = slice(s * BLOCK, (s + 1) * BLOCK)
        qt = q_all[rows, :].T
        cos, sin = cos_ref[:, rows], sin_ref[:, rows]
        for hd in range(N_Q_HEADS):
            x = qt[hd * HEAD_DIM:(hd + 1) * HEAD_DIM, :]
            xn = x * lax.rsqrt(jnp.mean(x * x, axis=0, keepdims=True) + EPS) * gain
            x1, x2 = xn[:half], xn[half:]
            r = jnp.concatenate([x1 * cos - x2 * sin, x2 * cos + x1 * sin], axis=0).astype(BF16)
            row0 = ((hd // Q_PER_KV) % 2) * HEAD_DIM
            qp_scr[s, hd // 2, row0:row0 + HEAD_DIM, (hd % 2) * LANES:(hd % 2 + 1) * LANES] = r
        if s == 0:
            k_prev = lambda l0: kp_ref[0, :, l0:l0 + LANES]
            bias = bias_ref[jnp.minimum(n, 1)]
        else:
            k_prev = lambda l0: kc_ref[0, (s - 1) * BLOCK:s * BLOCK, l0:l0 + LANES]
            bias = bias_ref[1]
        out = []
        for p in range(HEAD_PAIRS):
            hk = p // (Q_PER_KV // 2)
            lane0 = (hk // 2) * LANES
            rhs = qp_scr[s, p]
            s_p = jnp.dot(k_prev(lane0), rhs, preferred_element_type=F32) + bias[:BLOCK]
            s_c = jnp.dot(kc_ref[0, rows, lane0:lane0 + LANES], rhs, preferred_element_type=F32) + bias[BLOCK:]
            s_m = jnp.dot(km_ref[:, lane0:lane0 + LANES], rhs, preferred_element_type=F32)
            out.append((s_p, s_c, s_m))
        return out

    def finish(s, scores):
        rows = slice(s * BLOCK, (s + 1) * BLOCK)
        for p in range(HEAD_PAIRS):
            hk = p // (Q_PER_KV // 2)
            s_p, s_c, s_m = scores[p]
            sink = sink_ref[p:p + 1, :]
            m = jnp.maximum(
                jnp.maximum(jnp.max(s_p, axis=0, keepdims=True), jnp.max(s_c, axis=0, keepdims=True)),
                jnp.maximum(jnp.max(s_m, axis=0, keepdims=True), sink))
            e_p = jnp.exp2(s_p - m)
            e_c = jnp.exp2(s_c - m)
            e_m = jnp.exp2(s_m - m)
            den = (jnp.sum(e_p, axis=0, keepdims=True) + jnp.sum(e_c, axis=0, keepdims=True)
                   + jnp.sum(e_m, axis=0, keepdims=True) + jnp.exp2(sink - m))
            v0 = hk * HEAD_DIM
            if s == 0:
                v_prev = vp_ref[0, v0:v0 + HEAD_DIM, :]
            else:
                v_prev = vc_ref[0, v0:v0 + HEAD_DIM, (s - 1) * BLOCK:s * BLOCK]
            ot = (jnp.dot(v_prev, e_p.astype(BF16), preferred_element_type=F32)
                  + jnp.dot(vc_ref[0, v0:v0 + HEAD_DIM, rows], e_c.astype(BF16), preferred_element_type=F32)
                  + jnp.dot(vm_ref[v0:v0 + HEAD_DIM, :], e_m.astype(BF16), preferred_element_type=F32))
            ot = ot * (1.0 / den)
            ot_scr[s, 2 * p * HEAD_DIM:(2 * p + 1) * HEAD_DIM, :] = ot[:, :LANES]
            ot_scr[s, (2 * p + 1) * HEAD_DIM:(2 * p + 2) * HEAD_DIM, :] = ot[:, LANES:]
        o = ot_scr[s].T.astype(BF16)
        o_ref[0, rows, :] = h_ref[0, rows, :] + jnp.dot(o, wo_ref[...], preferred_element_type=F32)

    scores = [scores_of(s) for s in range(ATTN_SUB)]
    for s in range(ATTN_SUB):
        finish(s, scores[s])


def _attn_tables(q_gain, sinks, pos):
    half = HEAD_DIM // 2
    freqs = ROPE_THETA ** (-jnp.arange(0, half, dtype=F32) * 2.0 / HEAD_DIM)
    ang = pos.astype(F32)[:, None] * freqs[None, :]
    cos_t, sin_t = jnp.cos(ang).T, jnp.sin(ang).T
    log2e = math.log2(math.e)
    gain = jnp.broadcast_to((q_gain.astype(F32) * (HEAD_DIM ** -0.5 * log2e))[:, None], (HEAD_DIM, LANES))
    kj = jnp.arange(BLOCK)[:, None]
    qi = jnp.arange(BLOCK)[None, :]
    neg = jnp.full((BLOCK, BLOCK), NEG_INF, F32)
    prev = jnp.where(kj > qi, 0.0, NEG_INF).astype(F32)
    cur = jnp.where(kj <= qi, 0.0, NEG_INF).astype(F32)
    bias = jnp.stack([jnp.concatenate([neg, cur]), jnp.concatenate([prev, cur])])
    bias = jnp.tile(bias, (1, 1, 2))
    sink_t = jnp.repeat((sinks.astype(F32) * log2e).reshape(HEAD_PAIRS, 2), LANES, axis=1)
    return cos_t, sin_t, gain, bias, sink_t


def _attn_call(h, g, wq, tables, k, vt, k_meta, vt_meta, wo):
    nb, seq, _ = h.shape
    qb = ATTN_SUB * BLOCK
    assert seq % qb == 0
    cos_t, sin_t, gain, bias, sink_t = tables
    prev = lambda n: jnp.maximum(ATTN_SUB * n - 1, 0)
    half = HEAD_DIM // 2
    return pl.pallas_call(
        _attn_body,
        grid=(nb, seq // qb),
        in_specs=[
            pl.BlockSpec((1, qb, D_MODEL), lambda b, n: (b, n, 0)),
            _const_spec((1, D_MODEL)),
            _const_spec((D_MODEL, D_MODEL)),
            _const_spec((HEAD_DIM, LANES)),
            pl.BlockSpec((half, qb), lambda b, n: (0, n)),
            pl.BlockSpec((half, qb), lambda b, n: (0, n)),
            _const_spec((2, 2 * BLOCK, 2 * LANES)),
            _const_spec((HEAD_PAIRS, 2 * LANES)),
            pl.BlockSpec((1, BLOCK, KV_WIDTH), lambda b, n: (b, prev(n), 0)),
            pl.BlockSpec((1, qb, KV_WIDTH), lambda b, n: (b, n, 0)),
            _const_spec((N_META, KV_WIDTH)),
            pl.BlockSpec((1, KV_WIDTH, BLOCK), lambda b, n: (b, 0, prev(n))),
            pl.BlockSpec((1, KV_WIDTH, qb), lambda b, n: (b, 0, n)),
            _const_spec((KV_WIDTH, N_META)),
            _const_spec((D_MODEL, D_MODEL)),
        ],
        out_specs=pl.BlockSpec((1, qb, D_MODEL), lambda b, n: (b, n, 0)),
        out_shape=jax.ShapeDtypeStruct((nb, seq, D_MODEL), F32),
        scratch_shapes=[pltpu.VMEM((ATTN_SUB, HEAD_PAIRS, LANES, 2 * LANES), BF16),
                        pltpu.VMEM((ATTN_SUB, D_MODEL, BLOCK), F32)],
        compiler_params=pltpu.CompilerParams(
            dimension_semantics=("arbitrary", "arbitrary"), vmem_limit_bytes=VMEM_LIMIT),
        name="swa_attention",
    )(h, g.reshape(1, D_MODEL), wq, gain, cos_t, sin_t, bias, sink_t, k, k, k_meta, vt, vt, vt_meta, wo)


def kernel(x, meta_tokens, ffn1_norm, ffn1_w_gate_up, ffn1_w_down, mix_norm, ffn2_norm, ffn2_w_gate_up,
           ffn2_w_down, ssm_w_in, ssm_lambda_re, ssm_lambda_im, ssm_b_re, ssm_b_im, ssm_c_re, ssm_c_im,
           ssm_log_step, ssm_d, ssm_w_out, kv_norm, w_kv, k_norm, attn_w_q, q_norm, attn_sinks, attn_w_o):
    nb, seq, _ = x.shape
    rows = nb * seq
    tm = 1024
    tt = 64
    assert seq % tm == 0 and seq % tt == 0
    bf = lambda w: w.astype(BF16)
    n_state = SSM_GROUPS * SSM_STATE

    wb, lam, wc = _s5_discretise(ssm_lambda_re[0], ssm_lambda_im[0], ssm_b_re[0], ssm_b_im[0],
                                 ssm_c_re[0], ssm_c_im[0], ssm_log_step[0])
    s5_w = (mix_norm[0], bf(ssm_w_in[0]))
    s5_w2 = (wb, lam, wc, ssm_d[0].astype(F32), bf(ssm_w_out[0]))
    ffn_w = ((bf(ffn1_w_gate_up), bf(ffn1_w_down)), (bf(ffn2_w_gate_up), bf(ffn2_w_down)))
    ffn = lambda h2d, which, layer, t: _ffn_call(
        h2d, (ffn1_norm, ffn2_norm)[which][layer], *ffn_w[which], layer, t)
    seg = _seg_matrix()
    w_kv_b = bf(w_kv)
    kg128 = jnp.tile(k_norm.astype(F32), LANES // HEAD_DIM).reshape(1, LANES)

    hm = ffn(meta_tokens.astype(F32), 0, 0, N_META)
    hm_b = jnp.broadcast_to(hm[None], (nb, N_META, D_MODEL))
    zeros = jnp.zeros((nb, n_state), F32)
    hm_b, x0r, x0i = _s5_call(hm_b, *s5_w, *s5_w2, zeros, zeros, N_META)
    hm = ffn(hm_b[0], 1, 0, N_META)
    cos_m, sin_m = _rope_tables(jnp.arange(N_META))
    k_meta, v_meta = _kv_call(hm, kv_norm, w_kv_b, kg128, cos_m, sin_m, seg)

    h = ffn(x.reshape(rows, D_MODEL), 0, 0, tm)
    h, _, _ = _s5_call(h.reshape(nb, seq, D_MODEL), *s5_w, *s5_w2, x0r, x0i, tt)
    h = ffn(h.reshape(rows, D_MODEL), 1, 0, tm)

    cos_t, sin_t = _rope_tables(N_META + jnp.arange(seq))
    h, k, vt = _ffn_call(h, ffn1_norm[1], *ffn_w[0], 1, tm,
                         kv=(kv_norm, w_kv_b, kg128, cos_t, sin_t, seg, seq))
    k = k.reshape(nb, seq, KV_WIDTH)
    tables = _attn_tables(q_norm[0], attn_sinks[0], N_META + jnp.arange(seq))
    h = _attn_call(h.reshape(nb, seq, D_MODEL), mix_norm[1], bf(attn_w_q[0]), tables,
                   k, vt, k_meta, v_meta.T, bf(attn_w_o[0]))
    h = ffn(h.reshape(rows, D_MODEL), 1, 1, tm)
    return h.reshape(nb, seq, D_MODEL)
```

```python
import functools
import math

import jax
import jax.numpy as jnp
from jax import lax
from jax.experimental import pallas as pl
from jax.experimental.pallas import tpu as pltpu

F32 = jnp.float32
BF16 = jnp.bfloat16

D_MODEL = 1024
D_FF = 2816
N_META = 16
SSM_WIDTH = 512
SSM_GROUP = 16
SSM_GROUPS = 32
SSM_STATE = 64
HEAD_DIM = 64
N_Q_HEADS = 16
N_KV_HEADS = 4
Q_PER_KV = 4
KV_WIDTH = N_KV_HEADS * HEAD_DIM
BLOCK = 128
ROPE_THETA = 10000.0
EPS = 1e-6
NEG_INF = -1e30

LANES = 128
SSM_QUARTERS = 4
Q_GROUPS = SSM_GROUPS // SSM_QUARTERS
Q_IN = Q_GROUPS * SSM_GROUP
Q_STATE = Q_GROUPS * SSM_STATE
MXU_WIDTH = 256
FFN_CHUNK = MXU_WIDTH
FFN_SUB = 256
S5_ROW_BLOCK = 256
VMEM_LIMIT = 56 * 1024 * 1024


def _rms(x, g):
    return x * lax.rsqrt(jnp.mean(x * x, axis=-1, keepdims=True) + EPS) * g


def _const_spec(shape):
    nd = len(shape)
    return pl.BlockSpec(shape, lambda *_: (0,) * nd, pipeline_mode=pl.Buffered(1))


def _ffn_body(*refs, sub, with_kv):
    if with_kv:
        (x_ref, g_ref, wgu_ref, wd_ref, gkv_ref, wkv_ref, kg_ref, cos_ref, sin_ref, seg_ref,
         o_ref, k_ref, vt_ref, act_ref) = refs
    else:
        x_ref, g_ref, wgu_ref, wd_ref, o_ref, act_ref = refs
    for s in range(x_ref.shape[0] // sub):
        rs = slice(s * sub, (s + 1) * sub)
        x = x_ref[rs, :]
        xhat = x * lax.rsqrt(jnp.mean(x * x, axis=-1, keepdims=True) + EPS)
        xn = (xhat * g_ref[...]).astype(BF16)
        if with_kv:
            k, v = _kv_compute((xhat * gkv_ref[...]).astype(BF16), wkv_ref[...], kg_ref[...],
                               cos_ref[rs, :], sin_ref[rs, :], seg_ref[...])
            k_ref[rs, :] = k
            vt_ref[0, :, rs] = v.T.astype(BF16)
        for c in range(D_FF // FFN_CHUNK):
            lo = c * FFN_CHUNK
            a = jnp.dot(xn, wgu_ref[:, lo:lo + FFN_CHUNK], preferred_element_type=F32)
            b = jnp.dot(xn, wgu_ref[:, D_FF + lo:D_FF + lo + FFN_CHUNK], preferred_element_type=F32)
            act_ref[rs, lo:lo + FFN_CHUNK] = (a * jax.nn.sigmoid(a) * b).astype(BF16)
        y = jnp.dot(act_ref[rs, :], wd_ref[...], preferred_element_type=F32)
        o_ref[rs, :] = x + 0.5 * y


def _ffn_call(h2d, g, wgu, wd, layer, tm, kv=None):
    rows = h2d.shape[0]
    assert rows % tm == 0
    sub = min(tm, FFN_SUB)
    layer_spec = lambda shape: pl.BlockSpec((None,) + shape, lambda i: (layer, 0, 0),
                                            pipeline_mode=pl.Buffered(1))
    row_spec = pl.BlockSpec((tm, D_MODEL), lambda i: (i, 0))
    in_specs = [row_spec, _const_spec((1, D_MODEL)), layer_spec((D_MODEL, 2 * D_FF)), layer_spec((D_FF, D_MODEL))]
    args = [h2d, g.reshape(1, D_MODEL), wgu, wd]
    out_specs, out_shape = row_spec, jax.ShapeDtypeStruct((rows, D_MODEL), F32)
    if kv is not None:
        kv_norm, w_kv, kg128, cos_t, sin_t, seg, seq = kv
        assert seq % tm == 0
        sb = seq // tm
        table_spec = pl.BlockSpec((tm, LANES), lambda i: (i % sb, 0))
        in_specs += [_const_spec((1, D_MODEL)), _const_spec((D_MODEL, 2 * KV_WIDTH)), _const_spec((1, LANES)),
                     table_spec, table_spec, _const_spec((LANES, LANES))]
        args += [kv_norm.reshape(1, D_MODEL), w_kv, kg128, cos_t, sin_t, seg]
        out_specs = [row_spec, pl.BlockSpec((tm, KV_WIDTH), lambda i: (i, 0)),
                     pl.BlockSpec((1, KV_WIDTH, tm), lambda i: (i // sb, 0, i % sb))]
        out_shape = [out_shape, jax.ShapeDtypeStruct((rows, KV_WIDTH), BF16),
                     jax.ShapeDtypeStruct((rows // seq, KV_WIDTH, seq), BF16)]
    return pl.pallas_call(
        functools.partial(_ffn_body, sub=sub, with_kv=kv is not None),
        grid=(rows // tm,),
        in_specs=in_specs,
        out_specs=out_specs,
        out_shape=out_shape,
        scratch_shapes=[pltpu.VMEM((tm, D_FF), BF16)],
        compiler_params=pltpu.CompilerParams(
            dimension_semantics=("arbitrary",), vmem_limit_bytes=VMEM_LIMIT),
        name="ffn_kv" if kv is not None else "ffn",
    )(*args)


def _s5_pitch(tt):
    return tt + 8


def _s5_body(h_ref, g_ref, win_ref, wb_ref, lam_ref, wc_ref, d_ref, wout_ref,
             x0r_ref, x0i_ref, o_ref, xfr_ref, xfi_ref, pit, u_tb, xs, y_tb, y_bt, st_r, st_i, *, nb, tt):
    rows = nb * tt
    pitch = _s5_pitch(tt)
    tiles = SSM_WIDTH // LANES

    @pl.when(pl.program_id(0) == 0)
    def _():
        st_r[...] = x0r_ref[...]
        st_i[...] = x0i_ref[...]

    bpr = max(1, min(nb, S5_ROW_BLOCK // tt))
    for b0 in range(0, nb, bpr):
        hb = h_ref[b0:b0 + bpr].reshape(bpr * tt, D_MODEL)
        u = jnp.dot(_rms(hb, g_ref[...]).astype(BF16), win_ref[...], preferred_element_type=F32)
        for bl in range(bpr):
            for j in range(tiles):
                pit[j, (b0 + bl) * pitch:(b0 + bl) * pitch + tt, :] = u[bl * tt:(bl + 1) * tt,
                                                                      j * LANES:(j + 1) * LANES]
    for t in range(tt):
        for j in range(tiles):
            u_tb[t * nb:(t + 1) * nb, j * LANES:(j + 1) * LANES] = pit[j, pl.ds(t, nb, stride=pitch), :]

    def expand(q):
        uq = u_tb[:, q * Q_IN:(q + 1) * Q_IN].astype(BF16)
        xs[q % 2] = jnp.dot(uq, wb_ref[q], preferred_element_type=F32)

    expand(0)
    for q in range(SSM_QUARTERS):
        if q + 1 < SSM_QUARTERS:
            expand(q + 1)
        buf = xs.at[q % 2]
        sl = slice(q * Q_STATE, (q + 1) * Q_STATE)
        ar = jnp.broadcast_to(lam_ref[0:1, sl], (nb, Q_STATE))
        ai = jnp.broadcast_to(lam_ref[1:2, sl], (nb, Q_STATE))
        xr, xi = st_r[:, sl], st_i[:, sl]
        for t in range(tt):
            r = slice(t * nb, (t + 1) * nb)
            nxr = ar * xr - ai * xi + buf[r, 0:Q_STATE]
            nxi = ar * xi + ai * xr + buf[r, Q_STATE:2 * Q_STATE]
            buf[r, 0:Q_STATE] = nxr
            buf[r, Q_STATE:2 * Q_STATE] = nxi
            xr, xi = nxr, nxi
        st_r[:, sl] = xr
        st_i[:, sl] = xi
        y_tb[:, q * Q_IN:(q + 1) * Q_IN] = jnp.dot(
            buf[...].astype(BF16), wc_ref[q], preferred_element_type=F32)

    d = d_ref[...]
    for t in range(tt):
        r = slice(t * nb, (t + 1) * nb)
        y = jax.nn.gelu(y_tb[r, :] + d * u_tb[r, :])
        for j in range(tiles):
            pit[j, pl.ds(t, nb, stride=pitch), :] = y[:, j * LANES:(j + 1) * LANES]
    for b0 in range(0, nb, bpr):
        for b in range(b0, b0 + bpr):
            for j in range(tiles):
                y_bt[b * tt:(b + 1) * tt, j * LANES:(j + 1) * LANES] = pit[j, b * pitch:b * pitch + tt,
                                                                       :].astype(BF16)
        z = jnp.dot(y_bt[b0 * tt:(b0 + bpr) * tt, :], wout_ref[...], preferred_element_type=F32)
        hb = h_ref[b0:b0 + bpr].reshape(bpr * tt, D_MODEL)
        out = hb + z[:, :D_MODEL] * jax.nn.sigmoid(z[:, D_MODEL:])
        o_ref[b0:b0 + bpr] = out.reshape(bpr, tt, D_MODEL)
    xfr_ref[...] = st_r[...]
    xfi_ref[...] = st_i[...]


def _s5_call(h, g, win, wb, lam, wc, d, wout, x0r, x0i, tt):
    nb, seq, _ = h.shape
    rows = nb * tt
    assert tt % 8 == 0 and nb % 8 == 0
    n_state = SSM_GROUPS * SSM_STATE
    body = functools.partial(_s5_body, nb=nb, tt=tt)
    return pl.pallas_call(
        body,
        grid=(seq // tt,),
        in_specs=[
            pl.BlockSpec((nb, tt, D_MODEL), lambda i: (0, i, 0)),
            _const_spec((1, D_MODEL)),
            _const_spec((D_MODEL, SSM_WIDTH)),
            _const_spec((SSM_QUARTERS, Q_IN, 2 * Q_STATE)),
            _const_spec((2, n_state)),
            _const_spec((SSM_QUARTERS, 2 * Q_STATE, Q_IN)),
            _const_spec((1, SSM_WIDTH)),
            _const_spec((SSM_WIDTH, 2 * D_MODEL)),
            _const_spec((nb, n_state)),
            _const_spec((nb, n_state)),
        ],
        out_specs=[
            pl.BlockSpec((nb, tt, D_MODEL), lambda i: (0, i, 0)),
            pl.BlockSpec((nb, n_state), lambda i: (0, 0)),
            pl.BlockSpec((nb, n_state), lambda i: (0, 0)),
        ],
        out_shape=[
            jax.ShapeDtypeStruct((nb, seq, D_MODEL), F32),
            jax.ShapeDtypeStruct((nb, n_state), F32),
            jax.ShapeDtypeStruct((nb, n_state), F32),
        ],
        scratch_shapes=[
            pltpu.VMEM((SSM_WIDTH // LANES, nb * _s5_pitch(tt), LANES), F32),
            pltpu.VMEM((rows, SSM_WIDTH), F32),
            pltpu.VMEM((2, rows, 2 * Q_STATE), F32),
            pltpu.VMEM((rows, SSM_WIDTH), F32),
            pltpu.VMEM((rows, SSM_WIDTH), BF16),
            pltpu.VMEM((nb, n_state), F32),
            pltpu.VMEM((nb, n_state), F32),
        ],
        compiler_params=pltpu.CompilerParams(
            dimension_semantics=("arbitrary",), vmem_limit_bytes=VMEM_LIMIT),
        name="s5_mixer",
    )(h, g.reshape(1, D_MODEL), win, wb, lam, wc, d.reshape(1, SSM_WIDTH), wout, x0r, x0i)


def _s5_discretise(lam_re, lam_im, b_re, b_im, c_re, c_im, log_step):
    lr, li = lam_re.astype(F32), lam_im.astype(F32)
    step = jnp.exp(log_step.astype(F32))[:, None]
    mag = jnp.exp(lr * step)
    ar = mag * jnp.cos(li * step)
    ai = mag * jnp.sin(li * step)
    den = lr * lr + li * li
    nr, ni = ar - 1.0, ai
    cr = (nr * lr + ni * li) / den
    ci = (ni * lr - nr * li) / den
    br, bi = b_re.astype(F32), b_im.astype(F32)
    bbar_r = cr[..., None] * br - ci[..., None] * bi
    bbar_i = cr[..., None] * bi + ci[..., None] * br
    eye = jnp.eye(Q_GROUPS, dtype=F32)

    def expand(bb):
        t = jnp.einsum('qgpc,gh->qgchp', bb.reshape(SSM_QUARTERS, Q_GROUPS, SSM_STATE, SSM_GROUP), eye)
        return t.reshape(SSM_QUARTERS, Q_IN, Q_STATE)

    def contract(cc):
        t = jnp.einsum('qgcp,gh->qgphc', cc.reshape(SSM_QUARTERS, Q_GROUPS, SSM_GROUP, SSM_STATE), eye)
        return t.reshape(SSM_QUARTERS, Q_STATE, Q_IN)

    wb = jnp.concatenate([expand(bbar_r), expand(bbar_i)], axis=-1).astype(BF16)
    wc = jnp.concatenate([contract(c_re.astype(F32)), -contract(c_im.astype(F32))], axis=1).astype(BF16)
    lam = jnp.stack([ar.reshape(-1), ai.reshape(-1)])
    return wb, lam, wc


def _head_norm_rope(x, gain, cos, sin_signed, seg):
    ms = jnp.dot((x * x).astype(BF16), seg, preferred_element_type=F32)
    xn = x * lax.rsqrt(ms + EPS) * gain
    lane = lax.broadcasted_iota(jnp.int32, xn.shape, 1)
    first_half = (lane % HEAD_DIM) < (HEAD_DIM // 2)
    partner = jnp.where(first_half,
                        pltpu.roll(xn, LANES - HEAD_DIM // 2, 1),
                        pltpu.roll(xn, HEAD_DIM // 2, 1))
    return xn * cos + partner * sin_signed


def _rope_tables(pos):
    half = HEAD_DIM // 2
    freqs = ROPE_THETA ** (-jnp.arange(0, half, dtype=F32) * 2.0 / HEAD_DIM)
    ang = pos.astype(F32)[:, None] * freqs[None, :]
    cos, sin = jnp.cos(ang), jnp.sin(ang)
    cos_t = jnp.tile(cos, (1, 2 * LANES // HEAD_DIM))
    sin_t = jnp.tile(jnp.concatenate([-sin, sin], axis=-1), (1, LANES // HEAD_DIM))
    return cos_t, sin_t


def _seg_matrix():
    i = jnp.arange(LANES)
    return ((i[:, None] // HEAD_DIM) == (i[None, :] // HEAD_DIM)).astype(F32).astype(BF16) * (1.0 / HEAD_DIM)


def _kv_compute(hn, w_kv, k_gain, cos, sin_signed, seg):
    kv = jnp.dot(hn, w_kv, preferred_element_type=F32)
    k = jnp.concatenate(
        [_head_norm_rope(kv[:, j * LANES:(j + 1) * LANES], k_gain, cos, sin_signed, seg)
         for j in range(KV_WIDTH // LANES)], axis=1)
    return k.astype(BF16), kv[:, KV_WIDTH:]


def _kv_body(h_ref, g_ref, w_ref, kg_ref, cos_ref, sin_ref, seg_ref, k_ref, v_ref):
    hn = _rms(h_ref[...], g_ref[...]).astype(BF16)
    k, v = _kv_compute(hn, w_ref[...], kg_ref[...], cos_ref[...], sin_ref[...], seg_ref[...])
    k_ref[...] = k
    v_ref[...] = v.astype(BF16)


def _kv_call(h2d, g, w_kv, kg128, cos_t, sin_t, seg):
    rows = h2d.shape[0]
    return pl.pallas_call(
        _kv_body,
        grid=(1,),
        in_specs=[
            _const_spec((rows, D_MODEL)),
            _const_spec((1, D_MODEL)),
            _const_spec((D_MODEL, 2 * KV_WIDTH)),
            _const_spec((1, LANES)),
            _const_spec((rows, LANES)),
            _const_spec((rows, LANES)),
            _const_spec((LANES, LANES)),
        ],
        out_specs=[pl.BlockSpec((rows, KV_WIDTH), lambda i: (0, 0)),
                   pl.BlockSpec((rows, KV_WIDTH), lambda i: (0, 0))],
        out_shape=[jax.ShapeDtypeStruct((rows, KV_WIDTH), BF16),
                   jax.ShapeDtypeStruct((rows, KV_WIDTH), BF16)],
        compiler_params=pltpu.CompilerParams(
            dimension_semantics=("arbitrary",), vmem_limit_bytes=VMEM_LIMIT),
        name="shared_kv",
    )(h2d, g.reshape(1, D_MODEL), w_kv, kg128, cos_t, sin_t, seg)


HEAD_PAIRS = N_Q_HEADS // 2
ATTN_SUB = 4
ATTN_UNIT_PAIRS = 2


def _attn_body(h_ref, g_ref, wq_ref, qg_ref, cos_ref, sin_ref, bias_ref, sink_ref,
               kp_ref, kc_ref, km_ref, vp_ref, vc_ref, vm_ref, wo_ref, o_ref, qp_scr, ot_scr):
    n = pl.program_id(1)

    @pl.when((pl.program_id(0) == 0) & (n == 0))
    def _():
        qp_scr[...] = jnp.zeros_like(qp_scr)

    gain = qg_ref[...]
    half = HEAD_DIM // 2
    hn = _rms(h_ref[0], g_ref[...]).astype(BF16)
    q_all = jnp.dot(hn, wq_ref[...], preferred_element_type=F32)

    def scores_of(unit):
        s, part = divmod(unit, HEAD_PAIRS // ATTN_UNIT_PAIRS)
        pairs = range(part * ATTN_UNIT_PAIRS, (part + 1) * ATTN_UNIT_PAIRS)
        rows = slice(s * BLOCK, (s + 1) * BLOCK)
        col0 = pairs[0] * 2 * HEAD_DIM
        qt = q_all[rows, col0:col0 + len(pairs) * 2 * HEAD_DIM].T
        cos, sin = cos_ref[:, rows], sin_ref[:, rows]
        for hd in range(2 * pairs[0], 2 * pairs[-1] + 2):
            x = qt[hd * HEAD_DIM - col0:(hd + 1) * HEAD_DIM - col0, :]
            xn = x * lax.rsqrt(jnp.mean(x * x, axis=0, keepdims=True) + EPS) * gain
            x1, x2 = xn[:half], xn[half:]
            r = jnp.concatenate([x1 * cos - x2 * sin, x2 * cos + x1 * sin], axis=0).astype(BF16)
            row0 = ((hd // Q_PER_KV) % 2) * HEAD_DIM
            qp_scr[s, hd // 2, row0:row0 + HEAD_DIM, (hd % 2) * LANES:(hd % 2 + 1) * LANES] = r
        if s == 0:
            k_prev = lambda l0: kp_ref[0, :, l0:l0 + LANES]
            bias = bias_ref[jnp.minimum(n, 1)]
        else:
            k_prev = lambda l0: kc_ref[0, (s - 1) * BLOCK:s * BLOCK, l0:l0 + LANES]
            bias = bias_ref[1]
        out = []
        for p in pairs:
            hk = p // (Q_PER_KV // 2)
            lane0 = (hk // 2) * LANES
            rhs = qp_scr[s, p]
            s_p = jnp.dot(k_prev(lane0), rhs, preferred_element_type=F32) + bias[:BLOCK]
            s_c = jnp.dot(kc_ref[0, rows, lane0:lane0 + LANES], rhs, preferred_element_type=F32) + bias[BLOCK:]
            s_m = jnp.dot(km_ref[:, lane0:lane0 + LANES], rhs, preferred_element_type=F32)
            out.append((s_p, s_c, s_m))
        return out

    def finish(unit, scores):
        s, part = divmod(unit, HEAD_PAIRS // ATTN_UNIT_PAIRS)
        rows = slice(s * BLOCK, (s + 1) * BLOCK)
        for i, p in enumerate(range(part * ATTN_UNIT_PAIRS, (part + 1) * ATTN_UNIT_PAIRS)):
            hk = p // (Q_PER_KV // 2)
            s_p, s_c, s_m = scores[i]
            sink = sink_ref[p:p + 1, :]
            m = jnp.maximum(
                jnp.maximum(jnp.max(s_p, axis=0, keepdims=True), jnp.max(s_c, axis=0, keepdims=True)),
                jnp.maximum(jnp.max(s_m, axis=0, keepdims=True), sink))
            e_p = jnp.exp2(s_p - m)
            e_c = jnp.exp2(s_c - m)
            e_m = jnp.exp2(s_m - m)
            den = (jnp.sum(e_p, axis=0, keepdims=True) + jnp.sum(e_c, axis=0, keepdims=True)
                   + jnp.sum(e_m, axis=0, keepdims=True) + jnp.exp2(sink - m))
            v0 = hk * HEAD_DIM
            if s == 0:
                v_prev = vp_ref[0, v0:v0 + HEAD_DIM, :]
            else:
                v_prev = vc_ref[0, v0:v0 + HEAD_DIM, (s - 1) * BLOCK:s * BLOCK]
            ot = (jnp.dot(v_prev, e_p.astype(BF16), preferred_element_type=F32)
                  + jnp.dot(vc_ref[0, v0:v0 + HEAD_DIM, rows], e_c.astype(BF16), preferred_element_type=F32)
                  + jnp.dot(vm_ref[v0:v0 + HEAD_DIM, :], e_m.astype(BF16), preferred_element_type=F32))
            ot = ot * (1.0 / den)
            ot_scr[s, 2 * p * HEAD_DIM:(2 * p + 1) * HEAD_DIM, :] = ot[:, :LANES]
            ot_scr[s, (2 * p + 1) * HEAD_DIM:(2 * p + 2) * HEAD_DIM, :] = ot[:, LANES:]
        if part + 1 == HEAD_PAIRS // ATTN_UNIT_PAIRS:
            o = ot_scr[s].T.astype(BF16)
            o_ref[0, rows, :] = h_ref[0, rows, :] + jnp.dot(o, wo_ref[...], preferred_element_type=F32)

    n_units = ATTN_SUB * HEAD_PAIRS // ATTN_UNIT_PAIRS
    scores = scores_of(0)
    for u in range(n_units):
        ahead = scores_of(u + 1) if u + 1 < n_units else None
        finish(u, scores)
        scores = ahead


def _attn_tables(q_gain, sinks, pos):
    half = HEAD_DIM // 2
    freqs = ROPE_THETA ** (-jnp.arange(0, half, dtype=F32) * 2.0 / HEAD_DIM)
    ang = pos.astype(F32)[:, None] * freqs[None, :]
    cos_t, sin_t = jnp.cos(ang).T, jnp.sin(ang).T
    log2e = math.log2(math.e)
    gain = jnp.broadcast_to((q_gain.astype(F32) * (HEAD_DIM ** -0.5 * log2e))[:, None], (HEAD_DIM, LANES))
    kj = jnp.arange(BLOCK)[:, None]
    qi = jnp.arange(BLOCK)[None, :]
    neg = jnp.full((BLOCK, BLOCK), NEG_INF, F32)
    prev = jnp.where(kj > qi, 0.0, NEG_INF).astype(F32)
    cur = jnp.where(kj <= qi, 0.0, NEG_INF).astype(F32)
    bias = jnp.stack([jnp.concatenate([neg, cur]), jnp.concatenate([prev, cur])])
    bias = jnp.tile(bias, (1, 1, 2))
    sink_t = jnp.repeat((sinks.astype(F32) * log2e).reshape(HEAD_PAIRS, 2), LANES, axis=1)
    return cos_t, sin_t, gain, bias, sink_t


def _attn_call(h, g, wq, tables, k, vt, k_meta, vt_meta, wo):
    nb, seq, _ = h.shape
    qb = ATTN_SUB * BLOCK
    assert seq % qb == 0
    cos_t, sin_t, gain, bias, sink_t = tables
    prev = lambda n: jnp.maximum(ATTN_SUB * n - 1, 0)
    half = HEAD_DIM // 2
    return pl.pallas_call(
        _attn_body,
        grid=(nb, seq // qb),
        in_specs=[
            pl.BlockSpec((1, qb, D_MODEL), lambda b, n: (b, n, 0)),
            _const_spec((1, D_MODEL)),
            _const_spec((D_MODEL, D_MODEL)),
            _const_spec((HEAD_DIM, LANES)),
            pl.BlockSpec((half, qb), lambda b, n: (0, n)),
            pl.BlockSpec((half, qb), lambda b, n: (0, n)),
            _const_spec((2, 2 * BLOCK, 2 * LANES)),
            _const_spec((HEAD_PAIRS, 2 * LANES)),
            pl.BlockSpec((1, BLOCK, KV_WIDTH), lambda b, n: (b, prev(n), 0)),
            pl.BlockSpec((1, qb, KV_WIDTH), lambda b, n: (b, n, 0)),
            _const_spec((N_META, KV_WIDTH)),
            pl.BlockSpec((1, KV_WIDTH, BLOCK), lambda b, n: (b, 0, prev(n))),
            pl.BlockSpec((1, KV_WIDTH, qb), lambda b, n: (b, 0, n)),
            _const_spec((KV_WIDTH, N_META)),
            _const_spec((D_MODEL, D_MODEL)),
        ],
        out_specs=pl.BlockSpec((1, qb, D_MODEL), lambda b, n: (b, n, 0)),
        out_shape=jax.ShapeDtypeStruct((nb, seq, D_MODEL), F32),
        scratch_shapes=[pltpu.VMEM((ATTN_SUB, HEAD_PAIRS, LANES, 2 * LANES), BF16),
                        pltpu.VMEM((ATTN_SUB, D_MODEL, BLOCK), F32)],
        compiler_params=pltpu.CompilerParams(
            dimension_semantics=("arbitrary", "arbitrary"), vmem_limit_bytes=VMEM_LIMIT),
        name="swa_attention",
    )(h, g.reshape(1, D_MODEL), wq, gain, cos_t, sin_t, bias, sink_t, k, k, k_meta, vt, vt, vt_meta, wo)


def kernel(x, meta_tokens, ffn1_norm, ffn1_w_gate_up, ffn1_w_down, mix_norm, ffn2_norm, ffn2_w_gate_up,
           ffn2_w_down, ssm_w_in, ssm_lambda_re, ssm_lambda_im, ssm_b_re, ssm_b_im, ssm_c_re, ssm_c_im,
           ssm_log_step, ssm_d, ssm_w_out, kv_norm, w_kv, k_norm, attn_w_q, q_norm, attn_sinks, attn_w_o):
    nb, seq, _ = x.shape
    rows = nb * seq
    tm = 1024
    tt = 64
    assert seq % tm == 0 and seq % tt == 0
    bf = lambda w: w.astype(BF16)
    n_state = SSM_GROUPS * SSM_STATE

    wb, lam, wc = _s5_discretise(ssm_lambda_re[0], ssm_lambda_im[0], ssm_b_re[0], ssm_b_im[0],
                                 ssm_c_re[0], ssm_c_im[0], ssm_log_step[0])
    s5_w = (mix_norm[0], bf(ssm_w_in[0]))
    s5_w2 = (wb, lam, wc, ssm_d[0].astype(F32), bf(ssm_w_out[0]))
    ffn_w = ((bf(ffn1_w_gate_up), bf(ffn1_w_down)), (bf(ffn2_w_gate_up), bf(ffn2_w_down)))
    ffn = lambda h2d, which, layer, t: _ffn_call(
        h2d, (ffn1_norm, ffn2_norm)[which][layer], *ffn_w[which], layer, t)
    seg = _seg_matrix()
    w_kv_b = bf(w_kv)
    kg128 = jnp.tile(k_norm.astype(F32), LANES // HEAD_DIM).reshape(1, LANES)

    hm = ffn(meta_tokens.astype(F32), 0, 0, N_META)
    hm_b = jnp.broadcast_to(hm[None], (nb, N_META, D_MODEL))
    zeros = jnp.zeros((nb, n_state), F32)
    hm_b, x0r, x0i = _s5_call(hm_b, *s5_w, *s5_w2, zeros, zeros, N_META)
    hm = ffn(hm_b[0], 1, 0, N_META)
    cos_m, sin_m = _rope_tables(jnp.arange(N_META))
    k_meta, v_meta = _kv_call(hm, kv_norm, w_kv_b, kg128, cos_m, sin_m, seg)

    h = ffn(x.reshape(rows, D_MODEL), 0, 0, tm)
    h, _, _ = _s5_call(h.reshape(nb, seq, D_MODEL), *s5_w, *s5_w2, x0r, x0i, tt)
    h = ffn(h.reshape(rows, D_MODEL), 1, 0, tm)

    cos_t, sin_t = _rope_tables(N_META + jnp.arange(seq))
    h, k, vt = _ffn_call(h, ffn1_norm[1], *ffn_w[0], 1, tm,
                         kv=(kv_norm, w_kv_b, kg128, cos_t, sin_t, seg, seq))
    k = k.reshape(nb, seq, KV_WIDTH)
    tables = _attn_tables(q_norm[0], attn_sinks[0], N_META + jnp.arange(seq))
    h = _attn_call(h.reshape(nb, seq, D_MODEL), mix_norm[1], bf(attn_w_q[0]), tables,
                   k, vt, k_meta, v_meta.T, bf(attn_w_o[0]))
    h = ffn(h.reshape(rows, D_MODEL), 1, 1, tm)
    return h.reshape(nb, seq, D_MODEL)
```

```python
import functools
import math

import jax
import jax.numpy as jnp
from jax import lax
from jax.experimental import pallas as pl
from jax.experimental.pallas import tpu as pltpu

F32 = jnp.float32
BF16 = jnp.bfloat16

D_MODEL = 1024
D_FF = 2816
N_META = 16
SSM_WIDTH = 512
SSM_GROUP = 16
SSM_GROUPS = 32
SSM_STATE = 64
HEAD_DIM = 64
N_Q_HEADS = 16
N_KV_HEADS = 4
Q_PER_KV = 4
KV_WIDTH = N_KV_HEADS * HEAD_DIM
BLOCK = 128
ROPE_THETA = 10000.0
EPS = 1e-6
NEG_INF = -1e30

LANES = 128
SSM_QUARTERS = 4
Q_GROUPS = SSM_GROUPS // SSM_QUARTERS
Q_IN = Q_GROUPS * SSM_GROUP
Q_STATE = Q_GROUPS * SSM_STATE
MXU_WIDTH = 256
FFN_CHUNK = MXU_WIDTH
FFN_SUB = 256
S5_ROW_BLOCK = 256
VMEM_LIMIT = 56 * 1024 * 1024


def _rms(x, g):
    return x * lax.rsqrt(jnp.mean(x * x, axis=-1, keepdims=True) + EPS) * g


def _const_spec(shape):
    nd = len(shape)
    return pl.BlockSpec(shape, lambda *_: (0,) * nd, pipeline_mode=pl.Buffered(1))


def _ffn_body(*refs, sub, with_kv):
    if with_kv:
        (x_ref, g_ref, wgu_ref, wd_ref, gkv_ref, wkv_ref, kg_ref, cos_ref, sin_ref,
         o_ref, k_ref, vt_ref, act_ref) = refs
    else:
        x_ref, g_ref, wgu_ref, wd_ref, o_ref, act_ref = refs
    for s in range(x_ref.shape[0] // sub):
        rs = slice(s * sub, (s + 1) * sub)
        x = x_ref[rs, :]
        xhat = x * lax.rsqrt(jnp.mean(x * x, axis=-1, keepdims=True) + EPS)
        xn = (xhat * g_ref[...]).astype(BF16)
        if with_kv:
            k, v = _kv_compute((xhat * gkv_ref[...]).astype(BF16), wkv_ref[...], kg_ref[...],
                               cos_ref[rs, :], sin_ref[rs, :])
            k_ref[rs, :] = k
            vt_ref[0, :, rs] = v.T.astype(BF16)
        for c in range(D_FF // FFN_CHUNK):
            lo = c * FFN_CHUNK
            a = jnp.dot(xn, wgu_ref[:, lo:lo + FFN_CHUNK], preferred_element_type=F32)
            b = jnp.dot(xn, wgu_ref[:, D_FF + lo:D_FF + lo + FFN_CHUNK], preferred_element_type=F32)
            act_ref[rs, lo:lo + FFN_CHUNK] = (a * jax.nn.sigmoid(a) * b).astype(BF16)
        y = jnp.dot(act_ref[rs, :], wd_ref[...], preferred_element_type=F32)
        o_ref[rs, :] = x + 0.5 * y


def _ffn_call(h2d, g, wgu, wd, layer, tm, kv=None):
    rows = h2d.shape[0]
    assert rows % tm == 0
    sub = min(tm, FFN_SUB)
    layer_spec = lambda shape: pl.BlockSpec((None,) + shape, lambda i: (layer, 0, 0),
                                            pipeline_mode=pl.Buffered(1))
    row_spec = pl.BlockSpec((tm, D_MODEL), lambda i: (i, 0))
    in_specs = [row_spec, _const_spec((1, D_MODEL)), layer_spec((D_MODEL, 2 * D_FF)), layer_spec((D_FF, D_MODEL))]
    args = [h2d, g.reshape(1, D_MODEL), wgu, wd]
    out_specs, out_shape = row_spec, jax.ShapeDtypeStruct((rows, D_MODEL), F32)
    if kv is not None:
        kv_norm, w_kv, kg128, cos_t, sin_t, seq = kv
        assert seq % tm == 0
        sb = seq // tm
        table_spec = pl.BlockSpec((tm, LANES), lambda i: (i % sb, 0))
        in_specs += [_const_spec((1, D_MODEL)), _const_spec((D_MODEL, 2 * KV_WIDTH)), _const_spec((1, LANES)),
                     table_spec, table_spec]
        args += [kv_norm.reshape(1, D_MODEL), w_kv, kg128, cos_t, sin_t]
        out_specs = [row_spec, pl.BlockSpec((tm, KV_WIDTH), lambda i: (i, 0)),
                     pl.BlockSpec((1, KV_WIDTH, tm), lambda i: (i // sb, 0, i % sb))]
        out_shape = [out_shape, jax.ShapeDtypeStruct((rows, KV_WIDTH), BF16),
                     jax.ShapeDtypeStruct((rows // seq, KV_WIDTH, seq), BF16)]
    return pl.pallas_call(
        functools.partial(_ffn_body, sub=sub, with_kv=kv is not None),
        grid=(rows // tm,),
        in_specs=in_specs,
        out_specs=out_specs,
        out_shape=out_shape,
        scratch_shapes=[pltpu.VMEM((tm, D_FF), BF16)],
        compiler_params=pltpu.CompilerParams(
            dimension_semantics=("arbitrary",), vmem_limit_bytes=VMEM_LIMIT),
        name="ffn_kv" if kv is not None else "ffn",
    )(*args)


def _s5_pitch(tt):
    return tt + 8


def _s5_body(h_ref, g_ref, win_ref, wb_ref, lam_ref, wc_ref, d_ref, wout_ref,
             x0r_ref, x0i_ref, o_ref, xfr_ref, xfi_ref, pit, u_tb, xs, y_tb, y_bt, st_r, st_i, *, nb, tt):
    rows = nb * tt
    pitch = _s5_pitch(tt)
    tiles = SSM_WIDTH // LANES

    @pl.when(pl.program_id(0) == 0)
    def _():
        st_r[...] = x0r_ref[...]
        st_i[...] = x0i_ref[...]

    bpr = max(1, min(nb, S5_ROW_BLOCK // tt))
    for b0 in range(0, nb, bpr):
        hb = h_ref[b0:b0 + bpr].reshape(bpr * tt, D_MODEL)
        u = jnp.dot(_rms(hb, g_ref[...]).astype(BF16), win_ref[...], preferred_element_type=F32)
        for bl in range(bpr):
            for j in range(tiles):
                pit[j, (b0 + bl) * pitch:(b0 + bl) * pitch + tt, :] = u[bl * tt:(bl + 1) * tt,
                                                                      j * LANES:(j + 1) * LANES]
    for t in range(tt):
        for j in range(tiles):
            u_tb[t * nb:(t + 1) * nb, j * LANES:(j + 1) * LANES] = pit[j, pl.ds(t, nb, stride=pitch), :]

    def expand(q):
        uq = u_tb[:, q * Q_IN:(q + 1) * Q_IN].astype(BF16)
        xs[q % 2] = jnp.dot(uq, wb_ref[q], preferred_element_type=F32)

    expand(0)
    for q in range(SSM_QUARTERS):
        if q + 1 < SSM_QUARTERS:
            expand(q + 1)
        buf = xs.at[q % 2]
        sl = slice(q * Q_STATE, (q + 1) * Q_STATE)
        ar = jnp.broadcast_to(lam_ref[0:1, sl], (nb, Q_STATE))
        ai = jnp.broadcast_to(lam_ref[1:2, sl], (nb, Q_STATE))
        xr, xi = st_r[:, sl], st_i[:, sl]
        for t in range(tt):
            r = slice(t * nb, (t + 1) * nb)
            nxr = ar * xr - ai * xi + buf[r, 0:Q_STATE]
            nxi = ar * xi + ai * xr + buf[r, Q_STATE:2 * Q_STATE]
            buf[r, 0:Q_STATE] = nxr
            buf[r, Q_STATE:2 * Q_STATE] = nxi
            xr, xi = nxr, nxi
        st_r[:, sl] = xr
        st_i[:, sl] = xi
        y_tb[:, q * Q_IN:(q + 1) * Q_IN] = jnp.dot(
            buf[...].astype(BF16), wc_ref[q], preferred_element_type=F32)

    d = d_ref[...]
    for t in range(tt):
        r = slice(t * nb, (t + 1) * nb)
        y = jax.nn.gelu(y_tb[r, :] + d * u_tb[r, :])
        for j in range(tiles):
            pit[j, pl.ds(t, nb, stride=pitch), :] = y[:, j * LANES:(j + 1) * LANES]
    for b0 in range(0, nb, bpr):
        for b in range(b0, b0 + bpr):
            for j in range(tiles):
                y_bt[b * tt:(b + 1) * tt, j * LANES:(j + 1) * LANES] = pit[j, b * pitch:b * pitch + tt,
                                                                       :].astype(BF16)
        z = jnp.dot(y_bt[b0 * tt:(b0 + bpr) * tt, :], wout_ref[...], preferred_element_type=F32)
        hb = h_ref[b0:b0 + bpr].reshape(bpr * tt, D_MODEL)
        out = hb + z[:, :D_MODEL] * jax.nn.sigmoid(z[:, D_MODEL:])
        o_ref[b0:b0 + bpr] = out.reshape(bpr, tt, D_MODEL)
    xfr_ref[...] = st_r[...]
    xfi_ref[...] = st_i[...]


def _s5_call(h, g, win, wb, lam, wc, d, wout, x0r, x0i, tt):
    nb, seq, _ = h.shape
    rows = nb * tt
    assert tt % 8 == 0 and nb % 8 == 0
    n_state = SSM_GROUPS * SSM_STATE
    body = functools.partial(_s5_body, nb=nb, tt=tt)
    return pl.pallas_call(
        body,
        grid=(seq // tt,),
        in_specs=[
            pl.BlockSpec((nb, tt, D_MODEL), lambda i: (0, i, 0)),
            _const_spec((1, D_MODEL)),
            _const_spec((D_MODEL, SSM_WIDTH)),
            _const_spec((SSM_QUARTERS, Q_IN, 2 * Q_STATE)),
            _const_spec((2, n_state)),
            _const_spec((SSM_QUARTERS, 2 * Q_STATE, Q_IN)),
            _const_spec((1, SSM_WIDTH)),
            _const_spec((SSM_WIDTH, 2 * D_MODEL)),
            _const_spec((nb, n_state)),
            _const_spec((nb, n_state)),
        ],
        out_specs=[
            pl.BlockSpec((nb, tt, D_MODEL), lambda i: (0, i, 0)),
            pl.BlockSpec((nb, n_state), lambda i: (0, 0)),
            pl.BlockSpec((nb, n_state), lambda i: (0, 0)),
        ],
        out_shape=[
            jax.ShapeDtypeStruct((nb, seq, D_MODEL), F32),
            jax.ShapeDtypeStruct((nb, n_state), F32),
            jax.ShapeDtypeStruct((nb, n_state), F32),
        ],
        scratch_shapes=[
            pltpu.VMEM((SSM_WIDTH // LANES, nb * _s5_pitch(tt), LANES), F32),
            pltpu.VMEM((rows, SSM_WIDTH), F32),
            pltpu.VMEM((2, rows, 2 * Q_STATE), F32),
            pltpu.VMEM((rows, SSM_WIDTH), F32),
            pltpu.VMEM((rows, SSM_WIDTH), BF16),
            pltpu.VMEM((nb, n_state), F32),
            pltpu.VMEM((nb, n_state), F32),
        ],
        compiler_params=pltpu.CompilerParams(
            dimension_semantics=("arbitrary",), vmem_limit_bytes=VMEM_LIMIT),
        name="s5_mixer",
    )(h, g.reshape(1, D_MODEL), win, wb, lam, wc, d.reshape(1, SSM_WIDTH), wout, x0r, x0i)


def _s5_discretise(lam_re, lam_im, b_re, b_im, c_re, c_im, log_step):
    lr, li = lam_re.astype(F32), lam_im.astype(F32)
    step = jnp.exp(log_step.astype(F32))[:, None]
    mag = jnp.exp(lr * step)
    ar = mag * jnp.cos(li * step)
    ai = mag * jnp.sin(li * step)
    den = lr * lr + li * li
    nr, ni = ar - 1.0, ai
    cr = (nr * lr + ni * li) / den
    ci = (ni * lr - nr * li) / den
    br, bi = b_re.astype(F32), b_im.astype(F32)
    bbar_r = cr[..., None] * br - ci[..., None] * bi
    bbar_i = cr[..., None] * bi + ci[..., None] * br
    eye = jnp.eye(Q_GROUPS, dtype=F32)

    def expand(bb):
        t = jnp.einsum('qgpc,gh->qgchp', bb.reshape(SSM_QUARTERS, Q_GROUPS, SSM_STATE, SSM_GROUP), eye)
        return t.reshape(SSM_QUARTERS, Q_IN, Q_STATE)

    def contract(cc):
        t = jnp.einsum('qgcp,gh->qgphc', cc.reshape(SSM_QUARTERS, Q_GROUPS, SSM_GROUP, SSM_STATE), eye)
        return t.reshape(SSM_QUARTERS, Q_STATE, Q_IN)

    wb = jnp.concatenate([expand(bbar_r), expand(bbar_i)], axis=-1).astype(BF16)
    wc = jnp.concatenate([contract(c_re.astype(F32)), -contract(c_im.astype(F32))], axis=1).astype(BF16)
    lam = jnp.stack([ar.reshape(-1), ai.reshape(-1)])
    return wb, lam, wc


def _head_norm_rope(x, gain, cos, sin_signed):
    lane = lax.broadcasted_iota(jnp.int32, x.shape, 1)
    ss = x * x
    shift = HEAD_DIM // 2
    while shift:
        partner = jnp.where((lane & shift) == 0, pltpu.roll(ss, LANES - shift, 1), pltpu.roll(ss, shift, 1))
        ss = ss + partner
        shift //= 2
    xn = x * lax.rsqrt(ss * (1.0 / HEAD_DIM) + EPS) * gain
    first_half = (lane % HEAD_DIM) < (HEAD_DIM // 2)
    partner = jnp.where(first_half,
                        pltpu.roll(xn, LANES - HEAD_DIM // 2, 1),
                        pltpu.roll(xn, HEAD_DIM // 2, 1))
    return xn * cos + partner * sin_signed


def _rope_tables(pos):
    half = HEAD_DIM // 2
    freqs = ROPE_THETA ** (-jnp.arange(0, half, dtype=F32) * 2.0 / HEAD_DIM)
    ang = pos.astype(F32)[:, None] * freqs[None, :]
    cos, sin = jnp.cos(ang), jnp.sin(ang)
    cos_t = jnp.tile(cos, (1, 2 * LANES // HEAD_DIM))
    sin_t = jnp.tile(jnp.concatenate([-sin, sin], axis=-1), (1, LANES // HEAD_DIM))
    return cos_t, sin_t


def _kv_compute(hn, w_kv, k_gain, cos, sin_signed):
    kv = jnp.dot(hn, w_kv, preferred_element_type=F32)
    k = jnp.concatenate(
        [_head_norm_rope(kv[:, j * LANES:(j + 1) * LANES], k_gain, cos, sin_signed)
         for j in range(KV_WIDTH // LANES)], axis=1)
    return k.astype(BF16), kv[:, KV_WIDTH:]


def _kv_body(h_ref, g_ref, w_ref, kg_ref, cos_ref, sin_ref, k_ref, v_ref):
    hn = _rms(h_ref[...], g_ref[...]).astype(BF16)
    k, v = _kv_compute(hn, w_ref[...], kg_ref[...], cos_ref[...], sin_ref[...])
    k_ref[...] = k
    v_ref[...] = v.astype(BF16)


def _kv_call(h2d, g, w_kv, kg128, cos_t, sin_t):
    rows = h2d.shape[0]
    return pl.pallas_call(
        _kv_body,
        grid=(1,),
        in_specs=[
            _const_spec((rows, D_MODEL)),
            _const_spec((1, D_MODEL)),
            _const_spec((D_MODEL, 2 * KV_WIDTH)),
            _const_spec((1, LANES)),
            _const_spec((rows, LANES)),
            _const_spec((rows, LANES)),
        ],
        out_specs=[pl.BlockSpec((rows, KV_WIDTH), lambda i: (0, 0)),
                   pl.BlockSpec((rows, KV_WIDTH), lambda i: (0, 0))],
        out_shape=[jax.ShapeDtypeStruct((rows, KV_WIDTH), BF16),
                   jax.ShapeDtypeStruct((rows, KV_WIDTH), BF16)],
        compiler_params=pltpu.CompilerParams(
            dimension_semantics=("arbitrary",), vmem_limit_bytes=VMEM_LIMIT),
        name="shared_kv",
    )(h2d, g.reshape(1, D_MODEL), w_kv, kg128, cos_t, sin_t)


HEAD_PAIRS = N_Q_HEADS // 2
ATTN_SUB = 8
ATTN_UNIT_PAIRS = 2


def _attn_body(h_ref, g_ref, wq_ref, qg_ref, cos_ref, sin_ref, bias_ref, sink_ref,
               kp_ref, kc_ref, km_ref, vp_ref, vc_ref, vm_ref, wo_ref, o_ref, qp_scr, ot_scr):
    n = pl.program_id(1)

    @pl.when((pl.program_id(0) == 0) & (n == 0))
    def _():
        qp_scr[...] = jnp.zeros_like(qp_scr)

    gain = qg_ref[...]
    half = HEAD_DIM // 2
    hn = _rms(h_ref[0], g_ref[...]).astype(BF16)
    q_all = jnp.dot(hn, wq_ref[...], preferred_element_type=F32)

    def scores_of(unit):
        s, part = divmod(unit, HEAD_PAIRS // ATTN_UNIT_PAIRS)
        pairs = range(part * ATTN_UNIT_PAIRS, (part + 1) * ATTN_UNIT_PAIRS)
        rows = slice(s * BLOCK, (s + 1) * BLOCK)
        col0 = pairs[0] * 2 * HEAD_DIM
        qt = q_all[rows, col0:col0 + len(pairs) * 2 * HEAD_DIM].T
        cos, sin = cos_ref[:, rows], sin_ref[:, rows]
        for hd in range(2 * pairs[0], 2 * pairs[-1] + 2):
            x = qt[hd * HEAD_DIM - col0:(hd + 1) * HEAD_DIM - col0, :]
            xn = x * lax.rsqrt(jnp.mean(x * x, axis=0, keepdims=True) + EPS) * gain
            x1, x2 = xn[:half], xn[half:]
            r = jnp.concatenate([x1 * cos - x2 * sin, x2 * cos + x1 * sin], axis=0).astype(BF16)
            row0 = ((hd // Q_PER_KV) % 2) * HEAD_DIM
            qp_scr[s, hd // 2, row0:row0 + HEAD_DIM, (hd % 2) * LANES:(hd % 2 + 1) * LANES] = r
        if s == 0:
            k_prev = lambda l0: kp_ref[0, :, l0:l0 + LANES]
            bias = bias_ref[jnp.minimum(n, 1)]
        else:
            k_prev = lambda l0: kc_ref[0, (s - 1) * BLOCK:s * BLOCK, l0:l0 + LANES]
            bias = bias_ref[1]
        out = []
        for p in pairs:
            hk = p // (Q_PER_KV // 2)
            lane0 = (hk // 2) * LANES
            rhs = qp_scr[s, p]
            s_p = jnp.dot(k_prev(lane0), rhs, preferred_element_type=F32) + bias[:BLOCK]
            s_c = jnp.dot(kc_ref[0, rows, lane0:lane0 + LANES], rhs, preferred_element_type=F32) + bias[BLOCK:]
            s_m = jnp.dot(km_ref[:, lane0:lane0 + LANES], rhs, preferred_element_type=F32)
            out.append((s_p, s_c, s_m))
        return out

    def finish(unit, scores):
        s, part = divmod(unit, HEAD_PAIRS // ATTN_UNIT_PAIRS)
        rows = slice(s * BLOCK, (s + 1) * BLOCK)
        for i, p in enumerate(range(part * ATTN_UNIT_PAIRS, (part + 1) * ATTN_UNIT_PAIRS)):
            hk = p // (Q_PER_KV // 2)
            s_p, s_c, s_m = scores[i]
            sink = sink_ref[p:p + 1, :]
            m = jnp.maximum(
                jnp.maximum(jnp.max(s_p, axis=0, keepdims=True), jnp.max(s_c, axis=0, keepdims=True)),
                jnp.maximum(jnp.max(s_m, axis=0, keepdims=True), sink))
            e_p = jnp.exp2(s_p - m)
            e_c = jnp.exp2(s_c - m)
            e_m = jnp.exp2(s_m - m)
            den = (jnp.sum(e_p, axis=0, keepdims=True) + jnp.sum(e_c, axis=0, keepdims=True)
                   + jnp.sum(e_m, axis=0, keepdims=True) + jnp.exp2(sink - m))
            v0 = hk * HEAD_DIM
            if s == 0:
                v_prev = vp_ref[0, v0:v0 + HEAD_DIM, :]
            else:
                v_prev = vc_ref[0, v0:v0 + HEAD_DIM, (s - 1) * BLOCK:s * BLOCK]
            ot = (jnp.dot(v_prev, e_p.astype(BF16), preferred_element_type=F32)
                  + jnp.dot(vc_ref[0, v0:v0 + HEAD_DIM, rows], e_c.astype(BF16), preferred_element_type=F32)
                  + jnp.dot(vm_ref[v0:v0 + HEAD_DIM, :], e_m.astype(BF16), preferred_element_type=F32))
            ot = ot * (1.0 / den)
            ot_scr[s, 2 * p * HEAD_DIM:(2 * p + 1) * HEAD_DIM, :] = ot[:, :LANES]
            ot_scr[s, (2 * p + 1) * HEAD_DIM:(2 * p + 2) * HEAD_DIM, :] = ot[:, LANES:]
        if part + 1 == HEAD_PAIRS // ATTN_UNIT_PAIRS:
            o = ot_scr[s].T.astype(BF16)
            o_ref[0, rows, :] = h_ref[0, rows, :] + jnp.dot(o, wo_ref[...], preferred_element_type=F32)

    n_units = ATTN_SUB * HEAD_PAIRS // ATTN_UNIT_PAIRS
    scores = scores_of(0)
    for u in range(n_units):
        ahead = scores_of(u + 1) if u + 1 < n_units else None
        finish(u, scores)
        scores = ahead


def _attn_tables(q_gain, sinks, pos):
    half = HEAD_DIM // 2
    freqs = ROPE_THETA ** (-jnp.arange(0, half, dtype=F32) * 2.0 / HEAD_DIM)
    ang = pos.astype(F32)[:, None] * freqs[None, :]
    cos_t, sin_t = jnp.cos(ang).T, jnp.sin(ang).T
    log2e = math.log2(math.e)
    gain = jnp.broadcast_to((q_gain.astype(F32) * (HEAD_DIM ** -0.5 * log2e))[:, None], (HEAD_DIM, LANES))
    kj = jnp.arange(BLOCK)[:, None]
    qi = jnp.arange(BLOCK)[None, :]
    neg = jnp.full((BLOCK, BLOCK), NEG_INF, F32)
    prev = jnp.where(kj > qi, 0.0, NEG_INF).astype(F32)
    cur = jnp.where(kj <= qi, 0.0, NEG_INF).astype(F32)
    bias = jnp.stack([jnp.concatenate([neg, cur]), jnp.concatenate([prev, cur])])
    bias = jnp.tile(bias, (1, 1, 2))
    sink_t = jnp.repeat((sinks.astype(F32) * log2e).reshape(HEAD_PAIRS, 2), LANES, axis=1)
    return cos_t, sin_t, gain, bias, sink_t


def _attn_call(h, g, wq, tables, k, vt, k_meta, vt_meta, wo):
    nb, seq, _ = h.shape
    qb = ATTN_SUB * BLOCK
    assert seq % qb == 0
    cos_t, sin_t, gain, bias, sink_t = tables
    prev = lambda n: jnp.maximum(ATTN_SUB * n - 1, 0)
    half = HEAD_DIM // 2
    return pl.pallas_call(
        _attn_body,
        grid=(nb, seq // qb),
        in_specs=[
            pl.BlockSpec((1, qb, D_MODEL), lambda b, n: (b, n, 0)),
            _const_spec((1, D_MODEL)),
            _const_spec((D_MODEL, D_MODEL)),
            _const_spec((HEAD_DIM, LANES)),
            pl.BlockSpec((half, qb), lambda b, n: (0, n)),
            pl.BlockSpec((half, qb), lambda b, n: (0, n)),
            _const_spec((2, 2 * BLOCK, 2 * LANES)),
            _const_spec((HEAD_PAIRS, 2 * LANES)),
            pl.BlockSpec((1, BLOCK, KV_WIDTH), lambda b, n: (b, prev(n), 0)),
            pl.BlockSpec((1, qb, KV_WIDTH), lambda b, n: (b, n, 0)),
            _const_spec((N_META, KV_WIDTH)),
            pl.BlockSpec((1, KV_WIDTH, BLOCK), lambda b, n: (b, 0, prev(n))),
            pl.BlockSpec((1, KV_WIDTH, qb), lambda b, n: (b, 0, n)),
            _const_spec((KV_WIDTH, N_META)),
            _const_spec((D_MODEL, D_MODEL)),
        ],
        out_specs=pl.BlockSpec((1, qb, D_MODEL), lambda b, n: (b, n, 0)),
        out_shape=jax.ShapeDtypeStruct((nb, seq, D_MODEL), F32),
        scratch_shapes=[pltpu.VMEM((ATTN_SUB, HEAD_PAIRS, LANES, 2 * LANES), BF16),
                        pltpu.VMEM((ATTN_SUB, D_MODEL, BLOCK), F32)],
        compiler_params=pltpu.CompilerParams(
            dimension_semantics=("arbitrary", "arbitrary"), vmem_limit_bytes=VMEM_LIMIT),
        name="swa_attention",
    )(h, g.reshape(1, D_MODEL), wq, gain, cos_t, sin_t, bias, sink_t, k, k, k_meta, vt, vt, vt_meta, wo)


def kernel(x, meta_tokens, ffn1_norm, ffn1_w_gate_up, ffn1_w_down, mix_norm, ffn2_norm, ffn2_w_gate_up,
           ffn2_w_down, ssm_w_in, ssm_lambda_re, ssm_lambda_im, ssm_b_re, ssm_b_im, ssm_c_re, ssm_c_im,
           ssm_log_step, ssm_d, ssm_w_out, kv_norm, w_kv, k_norm, attn_w_q, q_norm, attn_sinks, attn_w_o):
    nb, seq, _ = x.shape
    rows = nb * seq
    tm = 1024
    tt = 64
    assert seq % tm == 0 and seq % tt == 0
    bf = lambda w: w.astype(BF16)
    n_state = SSM_GROUPS * SSM_STATE

    wb, lam, wc = _s5_discretise(ssm_lambda_re[0], ssm_lambda_im[0], ssm_b_re[0], ssm_b_im[0],
                                 ssm_c_re[0], ssm_c_im[0], ssm_log_step[0])
    s5_w = (mix_norm[0], bf(ssm_w_in[0]))
    s5_w2 = (wb, lam, wc, ssm_d[0].astype(F32), bf(ssm_w_out[0]))
    ffn_w = ((bf(ffn1_w_gate_up), bf(ffn1_w_down)), (bf(ffn2_w_gate_up), bf(ffn2_w_down)))
    ffn = lambda h2d, which, layer, t: _ffn_call(
        h2d, (ffn1_norm, ffn2_norm)[which][layer], *ffn_w[which], layer, t)
    w_kv_b = bf(w_kv)
    kg128 = jnp.tile(k_norm.astype(F32), LANES // HEAD_DIM).reshape(1, LANES)

    hm = ffn(meta_tokens.astype(F32), 0, 0, N_META)
    hm_b = jnp.broadcast_to(hm[None], (nb, N_META, D_MODEL))
    zeros = jnp.zeros((nb, n_state), F32)
    hm_b, x0r, x0i = _s5_call(hm_b, *s5_w, *s5_w2, zeros, zeros, N_META)
    hm = ffn(hm_b[0], 1, 0, N_META)
    cos_m, sin_m = _rope_tables(jnp.arange(N_META))
    k_meta, v_meta = _kv_call(hm, kv_norm, w_kv_b, kg128, cos_m, sin_m)

    h = ffn(x.reshape(rows, D_MODEL), 0, 0, tm)
    h, _, _ = _s5_call(h.reshape(nb, seq, D_MODEL), *s5_w, *s5_w2, x0r, x0i, tt)
    h = ffn(h.reshape(rows, D_MODEL), 1, 0, tm)

    cos_t, sin_t = _rope_tables(N_META + jnp.arange(seq))
    h, k, vt = _ffn_call(h, ffn1_norm[1], *ffn_w[0], 1, tm,
                         kv=(kv_norm, w_kv_b, kg128, cos_t, sin_t, seq))
    k = k.reshape(nb, seq, KV_WIDTH)
    tables = _attn_tables(q_norm[0], attn_sinks[0], N_META + jnp.arange(seq))
    h = _attn_call(h.reshape(nb, seq, D_MODEL), mix_norm[1], bf(attn_w_q[0]), tables,
                   k, vt, k_meta, v_meta.T, bf(attn_w_o[0]))
    h = ffn(h.reshape(rows, D_MODEL), 1, 1, tm)
    return h.reshape(nb, seq, D_MODEL)
```

```python
import functools
import math

import jax
import jax.numpy as jnp
from jax import lax
from jax.experimental import pallas as pl
from jax.experimental.pallas import tpu as pltpu

F32 = jnp.float32
BF16 = jnp.bfloat16

D_MODEL = 1024
D_FF = 2816
N_META = 16
SSM_WIDTH = 512
SSM_GROUP = 16
SSM_GROUPS = 32
SSM_STATE = 64
HEAD_DIM = 64
N_Q_HEADS = 16
N_KV_HEADS = 4
Q_PER_KV = 4
KV_WIDTH = N_KV_HEADS * HEAD_DIM
BLOCK = 128
ROPE_THETA = 10000.0
EPS = 1e-6
NEG_INF = -1e30

LANES = 128
SSM_QUARTERS = 4
Q_GROUPS = SSM_GROUPS // SSM_QUARTERS
Q_IN = Q_GROUPS * SSM_GROUP
Q_STATE = Q_GROUPS * SSM_STATE
MXU_WIDTH = 256
FFN_CHUNK = MXU_WIDTH
FFN_SUB = 256
S5_ROW_BLOCK = 256
VMEM_LIMIT = 56 * 1024 * 1024


def _rms(x, g):
    return x * lax.rsqrt(jnp.mean(x * x, axis=-1, keepdims=True) + EPS) * g


def _const_spec(shape):
    nd = len(shape)
    return pl.BlockSpec(shape, lambda *_: (0,) * nd, pipeline_mode=pl.Buffered(1))


def _ffn_body(*refs, sub, with_kv):
    if with_kv:
        (x_ref, g_ref, wgu_ref, wd_ref, gkv_ref, wkv_ref, kg_ref, cos_ref, sin_ref,
         o_ref, k_ref, vt_ref, act_ref) = refs
    else:
        x_ref, g_ref, wgu_ref, wd_ref, o_ref, act_ref = refs
    for s in range(x_ref.shape[0] // sub):
        rs = slice(s * sub, (s + 1) * sub)
        x = x_ref[rs, :]
        xhat = x * lax.rsqrt(jnp.mean(x * x, axis=-1, keepdims=True) + EPS)
        xn = (xhat * g_ref[...]).astype(BF16)
        if with_kv:
            k, v = _kv_compute((xhat * gkv_ref[...]).astype(BF16), wkv_ref[...], kg_ref[...],
                               cos_ref[rs, :], sin_ref[rs, :])
            k_ref[rs, :] = k
            vt_ref[0, :, rs] = v.T.astype(BF16)
        for c in range(D_FF // FFN_CHUNK):
            lo = c * FFN_CHUNK
            a = jnp.dot(xn, wgu_ref[:, lo:lo + FFN_CHUNK], preferred_element_type=F32)
            b = jnp.dot(xn, wgu_ref[:, D_FF + lo:D_FF + lo + FFN_CHUNK], preferred_element_type=F32)
            act_ref[rs, lo:lo + FFN_CHUNK] = (a * jax.nn.sigmoid(a) * b).astype(BF16)
        y = jnp.dot(act_ref[rs, :], wd_ref[...], preferred_element_type=F32)
        o_ref[rs, :] = x + 0.5 * y


def _ffn_call(h2d, g, wgu, wd, layer, tm, kv=None):
    rows = h2d.shape[0]
    assert rows % tm == 0
    sub = min(tm, FFN_SUB)
    layer_spec = lambda shape: pl.BlockSpec((None,) + shape, lambda i: (layer, 0, 0),
                                            pipeline_mode=pl.Buffered(1))
    row_spec = pl.BlockSpec((tm, D_MODEL), lambda i: (i, 0))
    in_specs = [row_spec, _const_spec((1, D_MODEL)), layer_spec((D_MODEL, 2 * D_FF)), layer_spec((D_FF, D_MODEL))]
    args = [h2d, g.reshape(1, D_MODEL), wgu, wd]
    out_specs, out_shape = row_spec, jax.ShapeDtypeStruct((rows, D_MODEL), F32)
    if kv is not None:
        kv_norm, w_kv, kg128, cos_t, sin_t, seq = kv
        assert seq % tm == 0
        sb = seq // tm
        table_spec = pl.BlockSpec((tm, LANES), lambda i: (i % sb, 0))
        in_specs += [_const_spec((1, D_MODEL)), _const_spec((D_MODEL, 2 * KV_WIDTH)), _const_spec((1, LANES)),
                     table_spec, table_spec]
        args += [kv_norm.reshape(1, D_MODEL), w_kv, kg128, cos_t, sin_t]
        out_specs = [row_spec, pl.BlockSpec((tm, KV_WIDTH), lambda i: (i, 0)),
                     pl.BlockSpec((1, KV_WIDTH, tm), lambda i: (i // sb, 0, i % sb))]
        out_shape = [out_shape, jax.ShapeDtypeStruct((rows, KV_WIDTH), BF16),
                     jax.ShapeDtypeStruct((rows // seq, KV_WIDTH, seq), BF16)]
    return pl.pallas_call(
        functools.partial(_ffn_body, sub=sub, with_kv=kv is not None),
        grid=(rows // tm,),
        in_specs=in_specs,
        out_specs=out_specs,
        out_shape=out_shape,
        scratch_shapes=[pltpu.VMEM((tm, D_FF), BF16)],
        compiler_params=pltpu.CompilerParams(
            dimension_semantics=("arbitrary",), vmem_limit_bytes=VMEM_LIMIT),
        name="ffn_kv" if kv is not None else "ffn",
    )(*args)


def _s5_pitch(tt):
    return tt + 8


def _s5_body(h_ref, g_ref, win_ref, wb_ref, lam_ref, wc_ref, d_ref, wout_ref,
             x0r_ref, x0i_ref, o_ref, xfr_ref, xfi_ref, pit, u_tb, xs, y_tb, y_bt, st_r, st_i, *, nb, tt):
    rows = nb * tt
    pitch = _s5_pitch(tt)
    tiles = SSM_WIDTH // LANES

    @pl.when(pl.program_id(0) == 0)
    def _():
        st_r[...] = x0r_ref[...]
        st_i[...] = x0i_ref[...]

    tb = max(8, min(tt, S5_ROW_BLOCK // nb))
    for t0 in range(0, tt, tb):
        hb = h_ref[:, t0:t0 + tb, :].reshape(nb * tb, D_MODEL)
        u = jnp.dot(_rms(hb, g_ref[...]).astype(BF16), win_ref[...], preferred_element_type=F32)
        for b in range(nb):
            for j in range(tiles):
                pit[j, b * pitch + t0:b * pitch + t0 + tb, :] = u[b * tb:(b + 1) * tb, j * LANES:(j + 1) * LANES]
        for t in range(t0, t0 + tb):
            for j in range(tiles):
                u_tb[t * nb:(t + 1) * nb, j * LANES:(j + 1) * LANES] = pit[j, pl.ds(t, nb, stride=pitch), :]

    def expand(q):
        uq = u_tb[:, q * Q_IN:(q + 1) * Q_IN].astype(BF16)
        xs[q % 2] = jnp.dot(uq, wb_ref[q], preferred_element_type=F32)

    expand(0)
    for q in range(SSM_QUARTERS):
        if q + 1 < SSM_QUARTERS:
            expand(q + 1)
        buf = xs.at[q % 2]
        sl = slice(q * Q_STATE, (q + 1) * Q_STATE)
        ar = jnp.broadcast_to(lam_ref[0:1, sl], (nb, Q_STATE))
        ai = jnp.broadcast_to(lam_ref[1:2, sl], (nb, Q_STATE))
        xr, xi = st_r[:, sl], st_i[:, sl]
        for t in range(tt):
            r = slice(t * nb, (t + 1) * nb)
            nxr = ar * xr - ai * xi + buf[r, 0:Q_STATE]
            nxi = ar * xi + ai * xr + buf[r, Q_STATE:2 * Q_STATE]
            buf[r, 0:Q_STATE] = nxr
            buf[r, Q_STATE:2 * Q_STATE] = nxi
            xr, xi = nxr, nxi
        st_r[:, sl] = xr
        st_i[:, sl] = xi
        y_tb[:, q * Q_IN:(q + 1) * Q_IN] = jnp.dot(
            buf[...].astype(BF16), wc_ref[q], preferred_element_type=F32)

    d = d_ref[...]
    for t0 in range(0, tt, tb):
        for t in range(t0, t0 + tb):
            r = slice(t * nb, (t + 1) * nb)
            y = jax.nn.gelu(y_tb[r, :] + d * u_tb[r, :])
            for j in range(tiles):
                pit[j, pl.ds(t, nb, stride=pitch), :] = y[:, j * LANES:(j + 1) * LANES]
        r0 = t0 * nb
        for b in range(nb):
            for j in range(tiles):
                y_bt[r0 + b * tb:r0 + (b + 1) * tb, j * LANES:(j + 1) * LANES] = (
                    pit[j, b * pitch + t0:b * pitch + t0 + tb, :].astype(BF16))
        z = jnp.dot(y_bt[r0:r0 + nb * tb, :], wout_ref[...], preferred_element_type=F32)
        hb = h_ref[:, t0:t0 + tb, :].reshape(nb * tb, D_MODEL)
        out = hb + z[:, :D_MODEL] * jax.nn.sigmoid(z[:, D_MODEL:])
        o_ref[:, t0:t0 + tb, :] = out.reshape(nb, tb, D_MODEL)
    xfr_ref[...] = st_r[...]
    xfi_ref[...] = st_i[...]


def _s5_call(h, g, win, wb, lam, wc, d, wout, x0r, x0i, tt):
    nb, seq, _ = h.shape
    rows = nb * tt
    assert tt % 8 == 0 and nb % 8 == 0
    n_state = SSM_GROUPS * SSM_STATE
    body = functools.partial(_s5_body, nb=nb, tt=tt)
    return pl.pallas_call(
        body,
        grid=(seq // tt,),
        in_specs=[
            pl.BlockSpec((nb, tt, D_MODEL), lambda i: (0, i, 0)),
            _const_spec((1, D_MODEL)),
            _const_spec((D_MODEL, SSM_WIDTH)),
            _const_spec((SSM_QUARTERS, Q_IN, 2 * Q_STATE)),
            _const_spec((2, n_state)),
            _const_spec((SSM_QUARTERS, 2 * Q_STATE, Q_IN)),
            _const_spec((1, SSM_WIDTH)),
            _const_spec((SSM_WIDTH, 2 * D_MODEL)),
            _const_spec((nb, n_state)),
            _const_spec((nb, n_state)),
        ],
        out_specs=[
            pl.BlockSpec((nb, tt, D_MODEL), lambda i: (0, i, 0)),
            pl.BlockSpec((nb, n_state), lambda i: (0, 0)),
            pl.BlockSpec((nb, n_state), lambda i: (0, 0)),
        ],
        out_shape=[
            jax.ShapeDtypeStruct((nb, seq, D_MODEL), F32),
            jax.ShapeDtypeStruct((nb, n_state), F32),
            jax.ShapeDtypeStruct((nb, n_state), F32),
        ],
        scratch_shapes=[
            pltpu.VMEM((SSM_WIDTH // LANES, nb * _s5_pitch(tt), LANES), F32),
            pltpu.VMEM((rows, SSM_WIDTH), F32),
            pltpu.VMEM((2, rows, 2 * Q_STATE), F32),
            pltpu.VMEM((rows, SSM_WIDTH), F32),
            pltpu.VMEM((rows, SSM_WIDTH), BF16),
            pltpu.VMEM((nb, n_state), F32),
            pltpu.VMEM((nb, n_state), F32),
        ],
        compiler_params=pltpu.CompilerParams(
            dimension_semantics=("arbitrary",), vmem_limit_bytes=VMEM_LIMIT),
        name="s5_mixer",
    )(h, g.reshape(1, D_MODEL), win, wb, lam, wc, d.reshape(1, SSM_WIDTH), wout, x0r, x0i)


def _s5_discretise(lam_re, lam_im, b_re, b_im, c_re, c_im, log_step):
    lr, li = lam_re.astype(F32), lam_im.astype(F32)
    step = jnp.exp(log_step.astype(F32))[:, None]
    mag = jnp.exp(lr * step)
    ar = mag * jnp.cos(li * step)
    ai = mag * jnp.sin(li * step)
    den = lr * lr + li * li
    nr, ni = ar - 1.0, ai
    cr = (nr * lr + ni * li) / den
    ci = (ni * lr - nr * li) / den
    br, bi = b_re.astype(F32), b_im.astype(F32)
    bbar_r = cr[..., None] * br - ci[..., None] * bi
    bbar_i = cr[..., None] * bi + ci[..., None] * br
    eye = jnp.eye(Q_GROUPS, dtype=F32)

    def expand(bb):
        t = jnp.einsum('qgpc,gh->qgchp', bb.reshape(SSM_QUARTERS, Q_GROUPS, SSM_STATE, SSM_GROUP), eye)
        return t.reshape(SSM_QUARTERS, Q_IN, Q_STATE)

    def contract(cc):
        t = jnp.einsum('qgcp,gh->qgphc', cc.reshape(SSM_QUARTERS, Q_GROUPS, SSM_GROUP, SSM_STATE), eye)
        return t.reshape(SSM_QUARTERS, Q_STATE, Q_IN)

    wb = jnp.concatenate([expand(bbar_r), expand(bbar_i)], axis=-1).astype(BF16)
    wc = jnp.concatenate([contract(c_re.astype(F32)), -contract(c_im.astype(F32))], axis=1).astype(BF16)
    lam = jnp.stack([ar.reshape(-1), ai.reshape(-1)])
    return wb, lam, wc


def _head_norm_rope(x, gain, cos, sin_signed):
    lane = lax.broadcasted_iota(jnp.int32, x.shape, 1)
    ss = x * x
    shift = HEAD_DIM // 2
    while shift:
        partner = jnp.where((lane & shift) == 0, pltpu.roll(ss, LANES - shift, 1), pltpu.roll(ss, shift, 1))
        ss = ss + partner
        shift //= 2
    xn = x * lax.rsqrt(ss * (1.0 / HEAD_DIM) + EPS) * gain
    first_half = (lane % HEAD_DIM) < (HEAD_DIM // 2)
    partner = jnp.where(first_half,
                        pltpu.roll(xn, LANES - HEAD_DIM // 2, 1),
                        pltpu.roll(xn, HEAD_DIM // 2, 1))
    return xn * cos + partner * sin_signed


def _rope_tables(pos):
    half = HEAD_DIM // 2
    freqs = ROPE_THETA ** (-jnp.arange(0, half, dtype=F32) * 2.0 / HEAD_DIM)
    ang = pos.astype(F32)[:, None] * freqs[None, :]
    cos, sin = jnp.cos(ang), jnp.sin(ang)
    cos_t = jnp.tile(cos, (1, 2 * LANES // HEAD_DIM))
    sin_t = jnp.tile(jnp.concatenate([-sin, sin], axis=-1), (1, LANES // HEAD_DIM))
    return cos_t, sin_t


def _kv_compute(hn, w_kv, k_gain, cos, sin_signed):
    kv = jnp.dot(hn, w_kv, preferred_element_type=F32)
    k = jnp.concatenate(
        [_head_norm_rope(kv[:, j * LANES:(j + 1) * LANES], k_gain, cos, sin_signed)
         for j in range(KV_WIDTH // LANES)], axis=1)
    return k.astype(BF16), kv[:, KV_WIDTH:]


def _kv_body(h_ref, g_ref, w_ref, kg_ref, cos_ref, sin_ref, k_ref, v_ref):
    hn = _rms(h_ref[...], g_ref[...]).astype(BF16)
    k, v = _kv_compute(hn, w_ref[...], kg_ref[...], cos_ref[...], sin_ref[...])
    k_ref[...] = k
    v_ref[...] = v.astype(BF16)


def _kv_call(h2d, g, w_kv, kg128, cos_t, sin_t):
    rows = h2d.shape[0]
    return pl.pallas_call(
        _kv_body,
        grid=(1,),
        in_specs=[
            _const_spec((rows, D_MODEL)),
            _const_spec((1, D_MODEL)),
            _const_spec((D_MODEL, 2 * KV_WIDTH)),
            _const_spec((1, LANES)),
            _const_spec((rows, LANES)),
            _const_spec((rows, LANES)),
        ],
        out_specs=[pl.BlockSpec((rows, KV_WIDTH), lambda i: (0, 0)),
                   pl.BlockSpec((rows, KV_WIDTH), lambda i: (0, 0))],
        out_shape=[jax.ShapeDtypeStruct((rows, KV_WIDTH), BF16),
                   jax.ShapeDtypeStruct((rows, KV_WIDTH), BF16)],
        compiler_params=pltpu.CompilerParams(
            dimension_semantics=("arbitrary",), vmem_limit_bytes=VMEM_LIMIT),
        name="shared_kv",
    )(h2d, g.reshape(1, D_MODEL), w_kv, kg128, cos_t, sin_t)


HEAD_PAIRS = N_Q_HEADS // 2
ATTN_SUB = 8
ATTN_UNIT_PAIRS = 2


def _attn_body(h_ref, g_ref, wq_ref, qg_ref, cos_ref, sin_ref, bias_ref, sink_ref,
               kp_ref, kc_ref, km_ref, vp_ref, vc_ref, vm_ref, wo_ref, o_ref, qp_scr, ot_scr):
    n = pl.program_id(1)

    @pl.when((pl.program_id(0) == 0) & (n == 0))
    def _():
        qp_scr[...] = jnp.zeros_like(qp_scr)

    gain = qg_ref[...]
    half = HEAD_DIM // 2
    hn = _rms(h_ref[0], g_ref[...]).astype(BF16)
    q_all = jnp.dot(hn, wq_ref[...], preferred_element_type=F32)

    def scores_of(unit):
        s, part = divmod(unit, HEAD_PAIRS // ATTN_UNIT_PAIRS)
        pairs = range(part * ATTN_UNIT_PAIRS, (part + 1) * ATTN_UNIT_PAIRS)
        rows = slice(s * BLOCK, (s + 1) * BLOCK)
        col0 = pairs[0] * 2 * HEAD_DIM
        qt = q_all[rows, col0:col0 + len(pairs) * 2 * HEAD_DIM].T
        cos, sin = cos_ref[:, rows], sin_ref[:, rows]
        for hd in range(2 * pairs[0], 2 * pairs[-1] + 2):
            x = qt[hd * HEAD_DIM - col0:(hd + 1) * HEAD_DIM - col0, :]
            xn = x * lax.rsqrt(jnp.mean(x * x, axis=0, keepdims=True) + EPS) * gain
            x1, x2 = xn[:half], xn[half:]
            r = jnp.concatenate([x1 * cos - x2 * sin, x2 * cos + x1 * sin], axis=0).astype(BF16)
            row0 = ((hd // Q_PER_KV) % 2) * HEAD_DIM
            qp_scr[s, hd // 2, row0:row0 + HEAD_DIM, (hd % 2) * LANES:(hd % 2 + 1) * LANES] = r
        if s == 0:
            k_prev = lambda l0: kp_ref[0, :, l0:l0 + LANES]
            bias = bias_ref[jnp.minimum(n, 1)]
        else:
            k_prev = lambda l0: kc_ref[0, (s - 1) * BLOCK:s * BLOCK, l0:l0 + LANES]
            bias = bias_ref[1]
        out = []
        for p in pairs:
            hk = p // (Q_PER_KV // 2)
            lane0 = (hk // 2) * LANES
            rhs = qp_scr[s, p]
            s_p = jnp.dot(k_prev(lane0), rhs, preferred_element_type=F32) + bias[:BLOCK]
            s_c = jnp.dot(kc_ref[0, rows, lane0:lane0 + LANES], rhs, preferred_element_type=F32) + bias[BLOCK:]
            s_m = jnp.dot(km_ref[:, lane0:lane0 + LANES], rhs, preferred_element_type=F32)
            out.append((s_p, s_c, s_m))
        return out

    def finish(unit, scores):
        s, part = divmod(unit, HEAD_PAIRS // ATTN_UNIT_PAIRS)
        rows = slice(s * BLOCK, (s + 1) * BLOCK)
        for i, p in enumerate(range(part * ATTN_UNIT_PAIRS, (part + 1) * ATTN_UNIT_PAIRS)):
            hk = p // (Q_PER_KV // 2)
            s_p, s_c, s_m = scores[i]
            sink = sink_ref[p:p + 1, :]
            m = jnp.maximum(
                jnp.maximum(jnp.max(s_p, axis=0, keepdims=True), jnp.max(s_c, axis=0, keepdims=True)),
                jnp.maximum(jnp.max(s_m, axis=0, keepdims=True), sink))
            e_p = jnp.exp2(s_p - m)
            e_c = jnp.exp2(s_c - m)
            e_m = jnp.exp2(s_m - m)
            den = (jnp.sum(e_p, axis=0, keepdims=True) + jnp.sum(e_c, axis=0, keepdims=True)
                   + jnp.sum(e_m, axis=0, keepdims=True) + jnp.exp2(sink - m))
            v0 = hk * HEAD_DIM
            if s == 0:
                v_prev = vp_ref[0, v0:v0 + HEAD_DIM, :]
            else:
                v_prev = vc_ref[0, v0:v0 + HEAD_DIM, (s - 1) * BLOCK:s * BLOCK]
            ot = (jnp.dot(v_prev, e_p.astype(BF16), preferred_element_type=F32)
                  + jnp.dot(vc_ref[0, v0:v0 + HEAD_DIM, rows], e_c.astype(BF16), preferred_element_type=F32)
                  + jnp.dot(vm_ref[v0:v0 + HEAD_DIM, :], e_m.astype(BF16), preferred_element_type=F32))
            ot = ot * (1.0 / den)
            ot_scr[s, 2 * p * HEAD_DIM:(2 * p + 1) * HEAD_DIM, :] = ot[:, :LANES]
            ot_scr[s, (2 * p + 1) * HEAD_DIM:(2 * p + 2) * HEAD_DIM, :] = ot[:, LANES:]
        if part + 1 == HEAD_PAIRS // ATTN_UNIT_PAIRS:
            o = ot_scr[s].T.astype(BF16)
            o_ref[0, rows, :] = h_ref[0, rows, :] + jnp.dot(o, wo_ref[...], preferred_element_type=F32)

    n_units = ATTN_SUB * HEAD_PAIRS // ATTN_UNIT_PAIRS
    scores = scores_of(0)
    for u in range(n_units):
        ahead = scores_of(u + 1) if u + 1 < n_units else None
        finish(u, scores)
        scores = ahead


def _attn_tables(q_gain, sinks, pos):
    half = HEAD_DIM // 2
    freqs = ROPE_THETA ** (-jnp.arange(0, half, dtype=F32) * 2.0 / HEAD_DIM)
    ang = pos.astype(F32)[:, None] * freqs[None, :]
    cos_t, sin_t = jnp.cos(ang).T, jnp.sin(ang).T
    log2e = math.log2(math.e)
    gain = jnp.broadcast_to((q_gain.astype(F32) * (HEAD_DIM ** -0.5 * log2e))[:, None], (HEAD_DIM, LANES))
    kj = jnp.arange(BLOCK)[:, None]
    qi = jnp.arange(BLOCK)[None, :]
    neg = jnp.full((BLOCK, BLOCK), NEG_INF, F32)
    prev = jnp.where(kj > qi, 0.0, NEG_INF).astype(F32)
    cur = jnp.where(kj <= qi, 0.0, NEG_INF).astype(F32)
    bias = jnp.stack([jnp.concatenate([neg, cur]), jnp.concatenate([prev, cur])])
    bias = jnp.tile(bias, (1, 1, 2))
    sink_t = jnp.repeat((sinks.astype(F32) * log2e).reshape(HEAD_PAIRS, 2), LANES, axis=1)
    return cos_t, sin_t, gain, bias, sink_t


def _attn_call(h, g, wq, tables, k, vt, k_meta, vt_meta, wo):
    nb, seq, _ = h.shape
    qb = ATTN_SUB * BLOCK
    assert seq % qb == 0
    cos_t, sin_t, gain, bias, sink_t = tables
    prev = lambda n: jnp.maximum(ATTN_SUB * n - 1, 0)
    half = HEAD_DIM // 2
    return pl.pallas_call(
        _attn_body,
        grid=(nb, seq // qb),
        in_specs=[
            pl.BlockSpec((1, qb, D_MODEL), lambda b, n: (b, n, 0)),
            _const_spec((1, D_MODEL)),
            _const_spec((D_MODEL, D_MODEL)),
            _const_spec((HEAD_DIM, LANES)),
            pl.BlockSpec((half, qb), lambda b, n: (0, n)),
            pl.BlockSpec((half, qb), lambda b, n: (0, n)),
            _const_spec((2, 2 * BLOCK, 2 * LANES)),
            _const_spec((HEAD_PAIRS, 2 * LANES)),
            pl.BlockSpec((1, BLOCK, KV_WIDTH), lambda b, n: (b, prev(n), 0)),
            pl.BlockSpec((1, qb, KV_WIDTH), lambda b, n: (b, n, 0)),
            _const_spec((N_META, KV_WIDTH)),
            pl.BlockSpec((1, KV_WIDTH, BLOCK), lambda b, n: (b, 0, prev(n))),
            pl.BlockSpec((1, KV_WIDTH, qb), lambda b, n: (b, 0, n)),
            _const_spec((KV_WIDTH, N_META)),
            _const_spec((D_MODEL, D_MODEL)),
        ],
        out_specs=pl.BlockSpec((1, qb, D_MODEL), lambda b, n: (b, n, 0)),
        out_shape=jax.ShapeDtypeStruct((nb, seq, D_MODEL), F32),
        scratch_shapes=[pltpu.VMEM((ATTN_SUB, HEAD_PAIRS, LANES, 2 * LANES), BF16),
                        pltpu.VMEM((ATTN_SUB, D_MODEL, BLOCK), F32)],
        compiler_params=pltpu.CompilerParams(
            dimension_semantics=("arbitrary", "arbitrary"), vmem_limit_bytes=VMEM_LIMIT),
        name="swa_attention",
    )(h, g.reshape(1, D_MODEL), wq, gain, cos_t, sin_t, bias, sink_t, k, k, k_meta, vt, vt, vt_meta, wo)


def kernel(x, meta_tokens, ffn1_norm, ffn1_w_gate_up, ffn1_w_down, mix_norm, ffn2_norm, ffn2_w_gate_up,
           ffn2_w_down, ssm_w_in, ssm_lambda_re, ssm_lambda_im, ssm_b_re, ssm_b_im, ssm_c_re, ssm_c_im,
           ssm_log_step, ssm_d, ssm_w_out, kv_norm, w_kv, k_norm, attn_w_q, q_norm, attn_sinks, attn_w_o):
    nb, seq, _ = x.shape
    rows = nb * seq
    tm = 1024
    tt = 64
    assert seq % tm == 0 and seq % tt == 0
    bf = lambda w: w.astype(BF16)
    n_state = SSM_GROUPS * SSM_STATE

    wb, lam, wc = _s5_discretise(ssm_lambda_re[0], ssm_lambda_im[0], ssm_b_re[0], ssm_b_im[0],
                                 ssm_c_re[0], ssm_c_im[0], ssm_log_step[0])
    s5_w = (mix_norm[0], bf(ssm_w_in[0]))
    s5_w2 = (wb, lam, wc, ssm_d[0].astype(F32), bf(ssm_w_out[0]))
    ffn_w = ((bf(ffn1_w_gate_up), bf(ffn1_w_down)), (bf(ffn2_w_gate_up), bf(ffn2_w_down)))
    ffn = lambda h2d, which, layer, t: _ffn_call(
        h2d, (ffn1_norm, ffn2_norm)[which][layer], *ffn_w[which], layer, t)
    w_kv_b = bf(w_kv)
    kg128 = jnp.tile(k_norm.astype(F32), LANES // HEAD_DIM).reshape(1, LANES)

    hm = ffn(meta_tokens.astype(F32), 0, 0, N_META)
    hm_b = jnp.broadcast_to(hm[None], (nb, N_META, D_MODEL))
    zeros = jnp.zeros((nb, n_state), F32)
    hm_b, x0r, x0i = _s5_call(hm_b, *s5_w, *s5_w2, zeros, zeros, N_META)
    hm = ffn(hm_b[0], 1, 0, N_META)
    cos_m, sin_m = _rope_tables(jnp.arange(N_META))
    k_meta, v_meta = _kv_call(hm, kv_norm, w_kv_b, kg128, cos_m, sin_m)

    h = ffn(x.reshape(rows, D_MODEL), 0, 0, tm)
    h, _, _ = _s5_call(h.reshape(nb, seq, D_MODEL), *s5_w, *s5_w2, x0r, x0i, tt)
    h = ffn(h.reshape(rows, D_MODEL), 1, 0, tm)

    cos_t, sin_t = _rope_tables(N_META + jnp.arange(seq))
    h, k, vt = _ffn_call(h, ffn1_norm[1], *ffn_w[0], 1, tm,
                         kv=(kv_norm, w_kv_b, kg128, cos_t, sin_t, seq))
    k = k.reshape(nb, seq, KV_WIDTH)
    tables = _attn_tables(q_norm[0], attn_sinks[0], N_META + jnp.arange(seq))
    h = _attn_call(h.reshape(nb, seq, D_MODEL), mix_norm[1], bf(attn_w_q[0]), tables,
                   k, vt, k_meta, v_meta.T, bf(attn_w_o[0]))
    h = ffn(h.reshape(rows, D_MODEL), 1, 1, tm)
    return h.reshape(nb, seq, D_MODEL)
```

```python
import functools
import math

import jax
import jax.numpy as jnp
from jax import lax
from jax.experimental import pallas as pl
from jax.experimental.pallas import tpu as pltpu

F32 = jnp.float32
BF16 = jnp.bfloat16

D_MODEL = 1024
D_FF = 2816
N_META = 16
SSM_WIDTH = 512
SSM_GROUP = 16
SSM_GROUPS = 32
SSM_STATE = 64
HEAD_DIM = 64
N_Q_HEADS = 16
N_KV_HEADS = 4
Q_PER_KV = 4
KV_WIDTH = N_KV_HEADS * HEAD_DIM
BLOCK = 128
ROPE_THETA = 10000.0
EPS = 1e-6
NEG_INF = -1e30

LANES = 128
SSM_QUARTERS = 4
Q_GROUPS = SSM_GROUPS // SSM_QUARTERS
Q_IN = Q_GROUPS * SSM_GROUP
Q_STATE = Q_GROUPS * SSM_STATE
MXU_WIDTH = 256
FFN_CHUNK = MXU_WIDTH
FFN_SUB = 256
S5_ROW_BLOCK = 128
VMEM_LIMIT = 56 * 1024 * 1024


def _rms(x, g):
    return x * lax.rsqrt(jnp.mean(x * x, axis=-1, keepdims=True) + EPS) * g


def _const_spec(shape):
    nd = len(shape)
    return pl.BlockSpec(shape, lambda *_: (0,) * nd, pipeline_mode=pl.Buffered(1))


def _ffn_body(*refs, sub, with_kv):
    if with_kv:
        (x_ref, g_ref, wgu_ref, wd_ref, gkv_ref, wkv_ref, kg_ref, cos_ref, sin_ref,
         o_ref, k_ref, vt_ref, act_ref) = refs
    else:
        x_ref, g_ref, wgu_ref, wd_ref, o_ref, act_ref = refs
    for s in range(x_ref.shape[0] // sub):
        rs = slice(s * sub, (s + 1) * sub)
        x = x_ref[rs, :]
        xhat = x * lax.rsqrt(jnp.mean(x * x, axis=-1, keepdims=True) + EPS)
        xn = (xhat * g_ref[...]).astype(BF16)
        if with_kv:
            k, v = _kv_compute((xhat * gkv_ref[...]).astype(BF16), wkv_ref[...], kg_ref[...],
                               cos_ref[rs, :], sin_ref[rs, :])
            k_ref[rs, :] = k
            vt_ref[0, :, rs] = v.T.astype(BF16)
        for c in range(D_FF // FFN_CHUNK):
            lo = c * FFN_CHUNK
            a = jnp.dot(xn, wgu_ref[:, lo:lo + FFN_CHUNK], preferred_element_type=F32)
            b = jnp.dot(xn, wgu_ref[:, D_FF + lo:D_FF + lo + FFN_CHUNK], preferred_element_type=F32)
            act_ref[rs, lo:lo + FFN_CHUNK] = (a * jax.nn.sigmoid(a) * b).astype(BF16)
        y = jnp.dot(act_ref[rs, :], wd_ref[...], preferred_element_type=F32)
        o_ref[rs, :] = x + 0.5 * y


def _ffn_call(h2d, g, wgu, wd, layer, tm, kv=None):
    rows = h2d.shape[0]
    assert rows % tm == 0
    sub = min(tm, FFN_SUB)
    layer_spec = lambda shape: pl.BlockSpec((None,) + shape, lambda i: (layer, 0, 0),
                                            pipeline_mode=pl.Buffered(1))
    row_spec = pl.BlockSpec((tm, D_MODEL), lambda i: (i, 0))
    in_specs = [row_spec, _const_spec((1, D_MODEL)), layer_spec((D_MODEL, 2 * D_FF)), layer_spec((D_FF, D_MODEL))]
    args = [h2d, g.reshape(1, D_MODEL), wgu, wd]
    out_specs, out_shape = row_spec, jax.ShapeDtypeStruct((rows, D_MODEL), F32)
    if kv is not None:
        kv_norm, w_kv, kg128, cos_t, sin_t, seq = kv
        assert seq % tm == 0
        sb = seq // tm
        table_spec = pl.BlockSpec((tm, LANES), lambda i: (i % sb, 0))
        in_specs += [_const_spec((1, D_MODEL)), _const_spec((D_MODEL, 2 * KV_WIDTH)), _const_spec((1, LANES)),
                     table_spec, table_spec]
        args += [kv_norm.reshape(1, D_MODEL), w_kv, kg128, cos_t, sin_t]
        out_specs = [row_spec, pl.BlockSpec((tm, KV_WIDTH), lambda i: (i, 0)),
                     pl.BlockSpec((1, KV_WIDTH, tm), lambda i: (i // sb, 0, i % sb))]
        out_shape = [out_shape, jax.ShapeDtypeStruct((rows, KV_WIDTH), BF16),
                     jax.ShapeDtypeStruct((rows // seq, KV_WIDTH, seq), BF16)]
    return pl.pallas_call(
        functools.partial(_ffn_body, sub=sub, with_kv=kv is not None),
        grid=(rows // tm,),
        in_specs=in_specs,
        out_specs=out_specs,
        out_shape=out_shape,
        scratch_shapes=[pltpu.VMEM((tm, D_FF), BF16)],
        compiler_params=pltpu.CompilerParams(
            dimension_semantics=("arbitrary",), vmem_limit_bytes=VMEM_LIMIT),
        name="ffn_kv" if kv is not None else "ffn",
    )(*args)


def _s5_pitch(tt):
    return tt + 8


def _s5_body(h_ref, g_ref, win_ref, wb_ref, lam_ref, wc_ref, d_ref, wout_ref,
             x0r_ref, x0i_ref, o_ref, xfr_ref, xfi_ref, pit, u_tb, xs, y_tb, y_bt, st_r, st_i, *, nb, tt):
    rows = nb * tt
    pitch = _s5_pitch(tt)
    tiles = SSM_WIDTH // LANES

    @pl.when(pl.program_id(0) == 0)
    def _():
        st_r[...] = x0r_ref[...]
        st_i[...] = x0i_ref[...]

    tb = max(8, min(tt, S5_ROW_BLOCK // nb))
    for t0 in range(0, tt, tb):
        hb = h_ref[:, t0:t0 + tb, :].reshape(nb * tb, D_MODEL)
        u = jnp.dot(_rms(hb, g_ref[...]).astype(BF16), win_ref[...], preferred_element_type=F32)
        for b in range(nb):
            for j in range(tiles):
                pit[j, b * pitch + t0:b * pitch + t0 + tb, :] = u[b * tb:(b + 1) * tb, j * LANES:(j + 1) * LANES]
        for t in range(t0, t0 + tb):
            for j in range(tiles):
                u_tb[t * nb:(t + 1) * nb, j * LANES:(j + 1) * LANES] = pit[j, pl.ds(t, nb, stride=pitch), :]

    def expand(q):
        uq = u_tb[:, q * Q_IN:(q + 1) * Q_IN].astype(BF16)
        xs[q % 2] = jnp.dot(uq, wb_ref[q], preferred_element_type=F32)

    expand(0)
    for q in range(SSM_QUARTERS):
        if q + 1 < SSM_QUARTERS:
            expand(q + 1)
        buf = xs.at[q % 2]
        sl = slice(q * Q_STATE, (q + 1) * Q_STATE)
        ar = jnp.broadcast_to(lam_ref[0:1, sl], (nb, Q_STATE))
        ai = jnp.broadcast_to(lam_ref[1:2, sl], (nb, Q_STATE))
        xr, xi = st_r[:, sl], st_i[:, sl]
        for t in range(tt):
            r = slice(t * nb, (t + 1) * nb)
            nxr = ar * xr - ai * xi + buf[r, 0:Q_STATE]
            nxi = ar * xi + ai * xr + buf[r, Q_STATE:2 * Q_STATE]
            buf[r, 0:Q_STATE] = nxr
            buf[r, Q_STATE:2 * Q_STATE] = nxi
            xr, xi = nxr, nxi
        st_r[:, sl] = xr
        st_i[:, sl] = xi
        y_tb[:, q * Q_IN:(q + 1) * Q_IN] = jnp.dot(
            buf[...].astype(BF16), wc_ref[q], preferred_element_type=F32)

    d = d_ref[...]
    for t0 in range(0, tt, tb):
        for t in range(t0, t0 + tb):
            r = slice(t * nb, (t + 1) * nb)
            y = jax.nn.gelu(y_tb[r, :] + d * u_tb[r, :])
            for j in range(tiles):
                pit[j, pl.ds(t, nb, stride=pitch), :] = y[:, j * LANES:(j + 1) * LANES]
        r0 = t0 * nb
        for b in range(nb):
            for j in range(tiles):
                y_bt[r0 + b * tb:r0 + (b + 1) * tb, j * LANES:(j + 1) * LANES] = (
                    pit[j, b * pitch + t0:b * pitch + t0 + tb, :].astype(BF16))
        z = jnp.dot(y_bt[r0:r0 + nb * tb, :], wout_ref[...], preferred_element_type=F32)
        hb = h_ref[:, t0:t0 + tb, :].reshape(nb * tb, D_MODEL)
        out = hb + z[:, :D_MODEL] * jax.nn.sigmoid(z[:, D_MODEL:])
        o_ref[:, t0:t0 + tb, :] = out.reshape(nb, tb, D_MODEL)
    xfr_ref[...] = st_r[...]
    xfi_ref[...] = st_i[...]


def _s5_call(h, g, win, wb, lam, wc, d, wout, x0r, x0i, tt):
    nb, seq, _ = h.shape
    rows = nb * tt
    assert tt % 8 == 0 and nb % 8 == 0
    n_state = SSM_GROUPS * SSM_STATE
    body = functools.partial(_s5_body, nb=nb, tt=tt)
    return pl.pallas_call(
        body,
        grid=(seq // tt,),
        in_specs=[
            pl.BlockSpec((nb, tt, D_MODEL), lambda i: (0, i, 0)),
            _const_spec((1, D_MODEL)),
            _const_spec((D_MODEL, SSM_WIDTH)),
            _const_spec((SSM_QUARTERS, Q_IN, 2 * Q_STATE)),
            _const_spec((2, n_state)),
            _const_spec((SSM_QUARTERS, 2 * Q_STATE, Q_IN)),
            _const_spec((1, SSM_WIDTH)),
            _const_spec((SSM_WIDTH, 2 * D_MODEL)),
            _const_spec((nb, n_state)),
            _const_spec((nb, n_state)),
        ],
        out_specs=[
            pl.BlockSpec((nb, tt, D_MODEL), lambda i: (0, i, 0)),
            pl.BlockSpec((nb, n_state), lambda i: (0, 0)),
            pl.BlockSpec((nb, n_state), lambda i: (0, 0)),
        ],
        out_shape=[
            jax.ShapeDtypeStruct((nb, seq, D_MODEL), F32),
            jax.ShapeDtypeStruct((nb, n_state), F32),
            jax.ShapeDtypeStruct((nb, n_state), F32),
        ],
        scratch_shapes=[
            pltpu.VMEM((SSM_WIDTH // LANES, nb * _s5_pitch(tt), LANES), F32),
            pltpu.VMEM((rows, SSM_WIDTH), F32),
            pltpu.VMEM((2, rows, 2 * Q_STATE), F32),
            pltpu.VMEM((rows, SSM_WIDTH), F32),
            pltpu.VMEM((rows, SSM_WIDTH), BF16),
            pltpu.VMEM((nb, n_state), F32),
            pltpu.VMEM((nb, n_state), F32),
        ],
        compiler_params=pltpu.CompilerParams(
            dimension_semantics=("arbitrary",), vmem_limit_bytes=VMEM_LIMIT),
        name="s5_mixer",
    )(h, g.reshape(1, D_MODEL), win, wb, lam, wc, d.reshape(1, SSM_WIDTH), wout, x0r, x0i)


def _s5_discretise(lam_re, lam_im, b_re, b_im, c_re, c_im, log_step):
    lr, li = lam_re.astype(F32), lam_im.astype(F32)
    step = jnp.exp(log_step.astype(F32))[:, None]
    mag = jnp.exp(lr * step)
    ar = mag * jnp.cos(li * step)
    ai = mag * jnp.sin(li * step)
    den = lr * lr + li * li
    nr, ni = ar - 1.0, ai
    cr = (nr * lr + ni * li) / den
    ci = (ni * lr - nr * li) / den
    br, bi = b_re.astype(F32), b_im.astype(F32)
    bbar_r = cr[..., None] * br - ci[..., None] * bi
    bbar_i = cr[..., None] * bi + ci[..., None] * br
    eye = jnp.eye(Q_GROUPS, dtype=F32)

    def expand(bb):
        t = jnp.einsum('qgpc,gh->qgchp', bb.reshape(SSM_QUARTERS, Q_GROUPS, SSM_STATE, SSM_GROUP), eye)
        return t.reshape(SSM_QUARTERS, Q_IN, Q_STATE)

    def contract(cc):
        t = jnp.einsum('qgcp,gh->qgphc', cc.reshape(SSM_QUARTERS, Q_GROUPS, SSM_GROUP, SSM_STATE), eye)
        return t.reshape(SSM_QUARTERS, Q_STATE, Q_IN)

    wb = jnp.concatenate([expand(bbar_r), expand(bbar_i)], axis=-1).astype(BF16)
    wc = jnp.concatenate([contract(c_re.astype(F32)), -contract(c_im.astype(F32))], axis=1).astype(BF16)
    lam = jnp.stack([ar.reshape(-1), ai.reshape(-1)])
    return wb, lam, wc


def _head_norm_rope(x, gain, cos, sin_signed):
    lane = lax.broadcasted_iota(jnp.int32, x.shape, 1)
    ss = x * x
    shift = HEAD_DIM // 2
    while shift:
        partner = jnp.where((lane & shift) == 0, pltpu.roll(ss, LANES - shift, 1), pltpu.roll(ss, shift, 1))
        ss = ss + partner
        shift //= 2
    xn = x * lax.rsqrt(ss * (1.0 / HEAD_DIM) + EPS) * gain
    first_half = (lane % HEAD_DIM) < (HEAD_DIM // 2)
    partner = jnp.where(first_half,
                        pltpu.roll(xn, LANES - HEAD_DIM // 2, 1),
                        pltpu.roll(xn, HEAD_DIM // 2, 1))
    return xn * cos + partner * sin_signed


def _rope_tables(pos):
    half = HEAD_DIM // 2
    freqs = ROPE_THETA ** (-jnp.arange(0, half, dtype=F32) * 2.0 / HEAD_DIM)
    ang = pos.astype(F32)[:, None] * freqs[None, :]
    cos, sin = jnp.cos(ang), jnp.sin(ang)
    cos_t = jnp.tile(cos, (1, 2 * LANES // HEAD_DIM))
    sin_t = jnp.tile(jnp.concatenate([-sin, sin], axis=-1), (1, LANES // HEAD_DIM))
    return cos_t, sin_t


def _kv_compute(hn, w_kv, k_gain, cos, sin_signed):
    kv = jnp.dot(hn, w_kv, preferred_element_type=F32)
    k = jnp.concatenate(
        [_head_norm_rope(kv[:, j * LANES:(j + 1) * LANES], k_gain, cos, sin_signed)
         for j in range(KV_WIDTH // LANES)], axis=1)
    return k.astype(BF16), kv[:, KV_WIDTH:]


def _kv_body(h_ref, g_ref, w_ref, kg_ref, cos_ref, sin_ref, k_ref, v_ref):
    hn = _rms(h_ref[...], g_ref[...]).astype(BF16)
    k, v = _kv_compute(hn, w_ref[...], kg_ref[...], cos_ref[...], sin_ref[...])
    k_ref[...] = k
    v_ref[...] = v.astype(BF16)


def _kv_call(h2d, g, w_kv, kg128, cos_t, sin_t):
    rows = h2d.shape[0]
    return pl.pallas_call(
        _kv_body,
        grid=(1,),
        in_specs=[
            _const_spec((rows, D_MODEL)),
            _const_spec((1, D_MODEL)),
            _const_spec((D_MODEL, 2 * KV_WIDTH)),
            _const_spec((1, LANES)),
            _const_spec((rows, LANES)),
            _const_spec((rows, LANES)),
        ],
        out_specs=[pl.BlockSpec((rows, KV_WIDTH), lambda i: (0, 0)),
                   pl.BlockSpec((rows, KV_WIDTH), lambda i: (0, 0))],
        out_shape=[jax.ShapeDtypeStruct((rows, KV_WIDTH), BF16),
                   jax.ShapeDtypeStruct((rows, KV_WIDTH), BF16)],
        compiler_params=pltpu.CompilerParams(
            dimension_semantics=("arbitrary",), vmem_limit_bytes=VMEM_LIMIT),
        name="shared_kv",
    )(h2d, g.reshape(1, D_MODEL), w_kv, kg128, cos_t, sin_t)


HEAD_PAIRS = N_Q_HEADS // 2
ATTN_SUB = 8
ATTN_UNIT_PAIRS = 2


def _attn_body(h_ref, g_ref, wq_ref, qg_ref, cos_ref, sin_ref, bias_ref, sink_ref,
               kp_ref, kc_ref, km_ref, vp_ref, vc_ref, vm_ref, wo_ref, o_ref, qp_scr, ot_scr):
    n = pl.program_id(1)

    @pl.when((pl.program_id(0) == 0) & (n == 0))
    def _():
        qp_scr[...] = jnp.zeros_like(qp_scr)

    gain = qg_ref[...]
    half = HEAD_DIM // 2
    hn = _rms(h_ref[0], g_ref[...]).astype(BF16)
    q_all = jnp.dot(hn, wq_ref[...], preferred_element_type=F32)

    def scores_of(unit):
        s, part = divmod(unit, HEAD_PAIRS // ATTN_UNIT_PAIRS)
        pairs = range(part * ATTN_UNIT_PAIRS, (part + 1) * ATTN_UNIT_PAIRS)
        rows = slice(s * BLOCK, (s + 1) * BLOCK)
        col0 = pairs[0] * 2 * HEAD_DIM
        qt = q_all[rows, col0:col0 + len(pairs) * 2 * HEAD_DIM].T
        cos, sin = cos_ref[:, rows], sin_ref[:, rows]
        for hd in range(2 * pairs[0], 2 * pairs[-1] + 2):
            x = qt[hd * HEAD_DIM - col0:(hd + 1) * HEAD_DIM - col0, :]
            xn = x * lax.rsqrt(jnp.mean(x * x, axis=0, keepdims=True) + EPS) * gain
            x1, x2 = xn[:half], xn[half:]
            r = jnp.concatenate([x1 * cos - x2 * sin, x2 * cos + x1 * sin], axis=0).astype(BF16)
            row0 = ((hd // Q_PER_KV) % 2) * HEAD_DIM
            qp_scr[s, hd // 2, row0:row0 + HEAD_DIM, (hd % 2) * LANES:(hd % 2 + 1) * LANES] = r
        if s == 0:
            k_prev = lambda l0: kp_ref[0, :, l0:l0 + LANES]
            bias = bias_ref[jnp.minimum(n, 1)]
        else:
            k_prev = lambda l0: kc_ref[0, (s - 1) * BLOCK:s * BLOCK, l0:l0 + LANES]
            bias = bias_ref[1]
        out = []
        for p in pairs:
            hk = p // (Q_PER_KV // 2)
            lane0 = (hk // 2) * LANES
            rhs = qp_scr[s, p]
            s_p = jnp.dot(k_prev(lane0), rhs, preferred_element_type=F32) + bias[:BLOCK]
            s_c = jnp.dot(kc_ref[0, rows, lane0:lane0 + LANES], rhs, preferred_element_type=F32) + bias[BLOCK:]
            s_m = jnp.dot(km_ref[:, lane0:lane0 + LANES], rhs, preferred_element_type=F32)
            out.append((s_p, s_c, s_m))
        return out

    def finish(unit, scores):
        s, part = divmod(unit, HEAD_PAIRS // ATTN_UNIT_PAIRS)
        rows = slice(s * BLOCK, (s + 1) * BLOCK)
        for i, p in enumerate(range(part * ATTN_UNIT_PAIRS, (part + 1) * ATTN_UNIT_PAIRS)):
            hk = p // (Q_PER_KV // 2)
            s_p, s_c, s_m = scores[i]
            sink = sink_ref[p:p + 1, :]
            m = jnp.maximum(
                jnp.maximum(jnp.max(s_p, axis=0, keepdims=True), jnp.max(s_c, axis=0, keepdims=True)),
                jnp.maximum(jnp.max(s_m, axis=0, keepdims=True), sink))
            e_p = jnp.exp2(s_p - m)
            e_c = jnp.exp2(s_c - m)
            e_m = jnp.exp2(s_m - m)
            den = (jnp.sum(e_p, axis=0, keepdims=True) + jnp.sum(e_c, axis=0, keepdims=True)
                   + jnp.sum(e_m, axis=0, keepdims=True) + jnp.exp2(sink - m))
            v0 = hk * HEAD_DIM
            if s == 0:
                v_prev = vp_ref[0, v0:v0 + HEAD_DIM, :]
            else:
                v_prev = vc_ref[0, v0:v0 + HEAD_DIM, (s - 1) * BLOCK:s * BLOCK]
            ot = (jnp.dot(v_prev, e_p.astype(BF16), preferred_element_type=F32)
                  + jnp.dot(vc_ref[0, v0:v0 + HEAD_DIM, rows], e_c.astype(BF16), preferred_element_type=F32)
                  + jnp.dot(vm_ref[v0:v0 + HEAD_DIM, :], e_m.astype(BF16), preferred_element_type=F32))
            ot = ot * (1.0 / den)
            ot_scr[s, 2 * p * HEAD_DIM:(2 * p + 1) * HEAD_DIM, :] = ot[:, :LANES]
            ot_scr[s, (2 * p + 1) * HEAD_DIM:(2 * p + 2) * HEAD_DIM, :] = ot[:, LANES:]
        if part + 1 == HEAD_PAIRS // ATTN_UNIT_PAIRS:
            o = ot_scr[s].T.astype(BF16)
            o_ref[0, rows, :] = h_ref[0, rows, :] + jnp.dot(o, wo_ref[...], preferred_element_type=F32)

    n_units = ATTN_SUB * HEAD_PAIRS // ATTN_UNIT_PAIRS
    scores = scores_of(0)
    for u in range(n_units):
        ahead = scores_of(u + 1) if u + 1 < n_units else None
        finish(u, scores)
        scores = ahead


def _attn_tables(q_gain, sinks, pos):
    half = HEAD_DIM // 2
    freqs = ROPE_THETA ** (-jnp.arange(0, half, dtype=F32) * 2.0 / HEAD_DIM)
    ang = pos.astype(F32)[:, None] * freqs[None, :]
    cos_t, sin_t = jnp.cos(ang).T, jnp.sin(ang).T
    log2e = math.log2(math.e)
    gain = jnp.broadcast_to((q_gain.astype(F32) * (HEAD_DIM ** -0.5 * log2e))[:, None], (HEAD_DIM, LANES))
    kj = jnp.arange(BLOCK)[:, None]
    qi = jnp.arange(BLOCK)[None, :]
    neg = jnp.full((BLOCK, BLOCK), NEG_INF, F32)
    prev = jnp.where(kj > qi, 0.0, NEG_INF).astype(F32)
    cur = jnp.where(kj <= qi, 0.0, NEG_INF).astype(F32)
    bias = jnp.stack([jnp.concatenate([neg, cur]), jnp.concatenate([prev, cur])])
    bias = jnp.tile(bias, (1, 1, 2))
    sink_t = jnp.repeat((sinks.astype(F32) * log2e).reshape(HEAD_PAIRS, 2), LANES, axis=1)
    return cos_t, sin_t, gain, bias, sink_t


def _attn_call(h, g, wq, tables, k, vt, k_meta, vt_meta, wo):
    nb, seq, _ = h.shape
    qb = ATTN_SUB * BLOCK
    assert seq % qb == 0
    cos_t, sin_t, gain, bias, sink_t = tables
    prev = lambda n: jnp.maximum(ATTN_SUB * n - 1, 0)
    half = HEAD_DIM // 2
    return pl.pallas_call(
        _attn_body,
        grid=(nb, seq // qb),
        in_specs=[
            pl.BlockSpec((1, qb, D_MODEL), lambda b, n: (b, n, 0)),
            _const_spec((1, D_MODEL)),
            _const_spec((D_MODEL, D_MODEL)),
            _const_spec((HEAD_DIM, LANES)),
            pl.BlockSpec((half, qb), lambda b, n: (0, n)),
            pl.BlockSpec((half, qb), lambda b, n: (0, n)),
            _const_spec((2, 2 * BLOCK, 2 * LANES)),
            _const_spec((HEAD_PAIRS, 2 * LANES)),
            pl.BlockSpec((1, BLOCK, KV_WIDTH), lambda b, n: (b, prev(n), 0)),
            pl.BlockSpec((1, qb, KV_WIDTH), lambda b, n: (b, n, 0)),
            _const_spec((N_META, KV_WIDTH)),
            pl.BlockSpec((1, KV_WIDTH, BLOCK), lambda b, n: (b, 0, prev(n))),
            pl.BlockSpec((1, KV_WIDTH, qb), lambda b, n: (b, 0, n)),
            _const_spec((KV_WIDTH, N_META)),
            _const_spec((D_MODEL, D_MODEL)),
        ],
        out_specs=pl.BlockSpec((1, qb, D_MODEL), lambda b, n: (b, n, 0)),
        out_shape=jax.ShapeDtypeStruct((nb, seq, D_MODEL), F32),
        scratch_shapes=[pltpu.VMEM((ATTN_SUB, HEAD_PAIRS, LANES, 2 * LANES), BF16),
                        pltpu.VMEM((ATTN_SUB, D_MODEL, BLOCK), F32)],
        compiler_params=pltpu.CompilerParams(
            dimension_semantics=("arbitrary", "arbitrary"), vmem_limit_bytes=VMEM_LIMIT),
        name="swa_attention",
    )(h, g.reshape(1, D_MODEL), wq, gain, cos_t, sin_t, bias, sink_t, k, k, k_meta, vt, vt, vt_meta, wo)


def kernel(x, meta_tokens, ffn1_norm, ffn1_w_gate_up, ffn1_w_down, mix_norm, ffn2_norm, ffn2_w_gate_up,
           ffn2_w_down, ssm_w_in, ssm_lambda_re, ssm_lambda_im, ssm_b_re, ssm_b_im, ssm_c_re, ssm_c_im,
           ssm_log_step, ssm_d, ssm_w_out, kv_norm, w_kv, k_norm, attn_w_q, q_norm, attn_sinks, attn_w_o):
    nb, seq, _ = x.shape
    rows = nb * seq
    tm = 1024
    tt = 64
    assert seq % tm == 0 and seq % tt == 0
    bf = lambda w: w.astype(BF16)
    n_state = SSM_GROUPS * SSM_STATE

    wb, lam, wc = _s5_discretise(ssm_lambda_re[0], ssm_lambda_im[0], ssm_b_re[0], ssm_b_im[0],
                                 ssm_c_re[0], ssm_c_im[0], ssm_log_step[0])
    s5_w = (mix_norm[0], bf(ssm_w_in[0]))
    s5_w2 = (wb, lam, wc, ssm_d[0].astype(F32), bf(ssm_w_out[0]))
    ffn_w = ((bf(ffn1_w_gate_up), bf(ffn1_w_down)), (bf(ffn2_w_gate_up), bf(ffn2_w_down)))
    ffn = lambda h2d, which, layer, t: _ffn_call(
        h2d, (ffn1_norm, ffn2_norm)[which][layer], *ffn_w[which], layer, t)
    w_kv_b = bf(w_kv)
    kg128 = jnp.tile(k_norm.astype(F32), LANES // HEAD_DIM).reshape(1, LANES)

    hm = ffn(meta_tokens.astype(F32), 0, 0, N_META)
    hm_b = jnp.broadcast_to(hm[None], (nb, N_META, D_MODEL))
    zeros = jnp.zeros((nb, n_state), F32)
    hm_b, x0r, x0i = _s5_call(hm_b, *s5_w, *s5_w2, zeros, zeros, N_META)
    hm = ffn(hm_b[0], 1, 0, N_META)
    cos_m, sin_m = _rope_tables(jnp.arange(N_META))
    k_meta, v_meta = _kv_call(hm, kv_norm, w_kv_b, kg128, cos_m, sin_m)

    h = ffn(x.reshape(rows, D_MODEL), 0, 0, tm)
    h, _, _ = _s5_call(h.reshape(nb, seq, D_MODEL), *s5_w, *s5_w2, x0r, x0i, tt)
    h = ffn(h.reshape(rows, D_MODEL), 1, 0, tm)

    cos_t, sin_t = _rope_tables(N_META + jnp.arange(seq))
    h, k, vt = _ffn_call(h, ffn1_norm[1], *ffn_w[0], 1, tm,
                         kv=(kv_norm, w_kv_b, kg128, cos_t, sin_t, seq))
    k = k.reshape(nb, seq, KV_WIDTH)
    tables = _attn_tables(q_norm[0], attn_sinks[0], N_META + jnp.arange(seq))
    h = _attn_call(h.reshape(nb, seq, D_MODEL), mix_norm[1], bf(attn_w_q[0]), tables,
                   k, vt, k_meta, v_meta.T, bf(attn_w_o[0]))
    h = ffn(h.reshape(rows, D_MODEL), 1, 1, tm)
    return h.reshape(nb, seq, D_MODEL)
```

```python
import functools
import math

import jax
import jax.numpy as jnp
from jax import lax
from jax.experimental import pallas as pl
from jax.experimental.pallas import tpu as pltpu

F32 = jnp.float32
BF16 = jnp.bfloat16

D_MODEL = 1024
D_FF = 2816
N_META = 16
SSM_WIDTH = 512
SSM_GROUP = 16
SSM_GROUPS = 32
SSM_STATE = 64
HEAD_DIM = 64
N_Q_HEADS = 16
N_KV_HEADS = 4
Q_PER_KV = 4
KV_WIDTH = N_KV_HEADS * HEAD_DIM
BLOCK = 128
ROPE_THETA = 10000.0
EPS = 1e-6
NEG_INF = -1e30

LANES = 128
SSM_QUARTERS = 4
Q_GROUPS = SSM_GROUPS // SSM_QUARTERS
Q_IN = Q_GROUPS * SSM_GROUP
Q_STATE = Q_GROUPS * SSM_STATE
MXU_WIDTH = 256
FFN_CHUNK = MXU_WIDTH
FFN_SUB = 256
S5_ROW_BLOCK = 128
VMEM_LIMIT = 56 * 1024 * 1024


def _rms(x, g):
    return x * lax.rsqrt(jnp.mean(x * x, axis=-1, keepdims=True) + EPS) * g


def _const_spec(shape):
    nd = len(shape)
    return pl.BlockSpec(shape, lambda *_: (0,) * nd, pipeline_mode=pl.Buffered(1))


def _ffn_body(*refs, sub, with_kv):
    if with_kv:
        (x_ref, g_ref, wgu_ref, wd_ref, gkv_ref, wkv_ref, kg_ref, cos_ref, sin_ref,
         o_ref, k_ref, vt_ref, act_ref) = refs
    else:
        x_ref, g_ref, wgu_ref, wd_ref, o_ref, act_ref = refs
    for s in range(x_ref.shape[0] // sub):
        rs = slice(s * sub, (s + 1) * sub)
        x = x_ref[rs, :]
        xhat = x * lax.rsqrt(jnp.mean(x * x, axis=-1, keepdims=True) + EPS)
        xn = (xhat * g_ref[...]).astype(BF16)
        if with_kv:
            k, v = _kv_compute((xhat * gkv_ref[...]).astype(BF16), wkv_ref[...], kg_ref[...],
                               cos_ref[rs, :], sin_ref[rs, :])
            k_ref[rs, :] = k
            vt_ref[0, :, rs] = v.T.astype(BF16)
        for c in range(D_FF // FFN_CHUNK):
            lo = c * FFN_CHUNK
            a = jnp.dot(xn, wgu_ref[:, lo:lo + FFN_CHUNK], preferred_element_type=F32)
            b = jnp.dot(xn, wgu_ref[:, D_FF + lo:D_FF + lo + FFN_CHUNK], preferred_element_type=F32)
            act_ref[rs, lo:lo + FFN_CHUNK] = (a * jax.nn.sigmoid(a) * b).astype(BF16)
        y = jnp.dot(act_ref[rs, :], wd_ref[...], preferred_element_type=F32)
        o_ref[rs, :] = x + 0.5 * y


def _ffn_call(h2d, g, wgu, wd, layer, tm, kv=None):
    rows = h2d.shape[0]
    assert rows % tm == 0
    sub = min(tm, FFN_SUB)
    layer_spec = lambda shape: pl.BlockSpec((None,) + shape, lambda i: (layer, 0, 0),
                                            pipeline_mode=pl.Buffered(1))
    row_spec = pl.BlockSpec((tm, D_MODEL), lambda i: (i, 0))
    in_specs = [row_spec, _const_spec((1, D_MODEL)), layer_spec((D_MODEL, 2 * D_FF)), layer_spec((D_FF, D_MODEL))]
    args = [h2d, g.reshape(1, D_MODEL), wgu, wd]
    out_specs, out_shape = row_spec, jax.ShapeDtypeStruct((rows, D_MODEL), F32)
    if kv is not None:
        kv_norm, w_kv, kg128, cos_t, sin_t, seq = kv
        assert seq % tm == 0
        sb = seq // tm
        table_spec = pl.BlockSpec((tm, LANES), lambda i: (i % sb, 0))
        in_specs += [_const_spec((1, D_MODEL)), _const_spec((D_MODEL, 2 * KV_WIDTH)), _const_spec((1, LANES)),
                     table_spec, table_spec]
        args += [kv_norm.reshape(1, D_MODEL), w_kv, kg128, cos_t, sin_t]
        out_specs = [row_spec, pl.BlockSpec((tm, KV_WIDTH), lambda i: (i, 0)),
                     pl.BlockSpec((1, KV_WIDTH, tm), lambda i: (i // sb, 0, i % sb))]
        out_shape = [out_shape, jax.ShapeDtypeStruct((rows, KV_WIDTH), BF16),
                     jax.ShapeDtypeStruct((rows // seq, KV_WIDTH, seq), BF16)]
    return pl.pallas_call(
        functools.partial(_ffn_body, sub=sub, with_kv=kv is not None),
        grid=(rows // tm,),
        in_specs=in_specs,
        out_specs=out_specs,
        out_shape=out_shape,
        scratch_shapes=[pltpu.VMEM((tm, D_FF), BF16)],
        compiler_params=pltpu.CompilerParams(
            dimension_semantics=("arbitrary",), vmem_limit_bytes=VMEM_LIMIT),
        name="ffn_kv" if kv is not None else "ffn",
    )(*args)


def _s5_pitch(tt):
    return tt + 8


def _s5_body(h_ref, g_ref, win_ref, wb_ref, lam_ref, wc_ref, d_ref, wout_ref,
             x0r_ref, x0i_ref, o_ref, xfr_ref, xfi_ref, pit, u_tb, xs, y_tb, y_bt, st_r, st_i, *, nb, tt):
    rows = nb * tt
    pitch = _s5_pitch(tt)
    tiles = SSM_WIDTH // LANES

    @pl.when(pl.program_id(0) == 0)
    def _():
        st_r[...] = x0r_ref[...]
        st_i[...] = x0i_ref[...]

    tb = max(8, min(tt, S5_ROW_BLOCK // nb))
    for t0 in range(0, tt, tb):
        hb = h_ref[:, t0:t0 + tb, :].reshape(nb * tb, D_MODEL)
        u = jnp.dot(_rms(hb, g_ref[...]).astype(BF16), win_ref[...], preferred_element_type=F32)
        for b in range(nb):
            for j in range(tiles):
                pit[j, b * pitch + t0:b * pitch + t0 + tb, :] = u[b * tb:(b + 1) * tb, j * LANES:(j + 1) * LANES]
        for t in range(t0, t0 + tb):
            for j in range(tiles):
                u_tb[t * nb:(t + 1) * nb, j * LANES:(j + 1) * LANES] = pit[j, pl.ds(t, nb, stride=pitch), :]

    def expand(q):
        uq = u_tb[:, q * Q_IN:(q + 1) * Q_IN].astype(BF16)
        xs[q % 2] = jnp.dot(uq, wb_ref[q], preferred_element_type=F32)

    expand(0)
    for q in range(SSM_QUARTERS):
        if q + 1 < SSM_QUARTERS:
            expand(q + 1)
        buf = xs.at[q % 2]
        sl = slice(q * Q_STATE, (q + 1) * Q_STATE)
        ar = jnp.broadcast_to(lam_ref[0:1, sl], (nb, Q_STATE))
        ai = jnp.broadcast_to(lam_ref[1:2, sl], (nb, Q_STATE))
        xr, xi = st_r[:, sl], st_i[:, sl]
        for t in range(tt):
            r = slice(t * nb, (t + 1) * nb)
            nxr = ar * xr - ai * xi + buf[r, 0:Q_STATE]
            nxi = ar * xi + ai * xr + buf[r, Q_STATE:2 * Q_STATE]
            buf[r, 0:Q_STATE] = nxr
            buf[r, Q_STATE:2 * Q_STATE] = nxi
            xr, xi = nxr, nxi
        st_r[:, sl] = xr
        st_i[:, sl] = xi
        y_tb[:, q * Q_IN:(q + 1) * Q_IN] = jnp.dot(
            buf[...].astype(BF16), wc_ref[q], preferred_element_type=F32)

    d = d_ref[...]
    for t0 in range(0, tt, tb):
        for t in range(t0, t0 + tb):
            r = slice(t * nb, (t + 1) * nb)
            y = jax.nn.gelu(y_tb[r, :] + d * u_tb[r, :])
            for j in range(tiles):
                pit[j, pl.ds(t, nb, stride=pitch), :] = y[:, j * LANES:(j + 1) * LANES]
        r0 = t0 * nb
        for b in range(nb):
            for j in range(tiles):
                y_bt[r0 + b * tb:r0 + (b + 1) * tb, j * LANES:(j + 1) * LANES] = (
                    pit[j, b * pitch + t0:b * pitch + t0 + tb, :].astype(BF16))
        z = jnp.dot(y_bt[r0:r0 + nb * tb, :], wout_ref[...], preferred_element_type=F32)
        hb = h_ref[:, t0:t0 + tb, :].reshape(nb * tb, D_MODEL)
        out = hb + z[:, :D_MODEL] * jax.nn.sigmoid(z[:, D_MODEL:])
        o_ref[:, t0:t0 + tb, :] = out.reshape(nb, tb, D_MODEL)
    xfr_ref[...] = st_r[...]
    xfi_ref[...] = st_i[...]


def _s5_call(h, g, win, wb, lam, wc, d, wout, x0r, x0i, tt):
    nb, seq, _ = h.shape
    rows = nb * tt
    assert tt % 8 == 0 and nb % 8 == 0
    n_state = SSM_GROUPS * SSM_STATE
    body = functools.partial(_s5_body, nb=nb, tt=tt)
    return pl.pallas_call(
        body,
        grid=(seq // tt,),
        in_specs=[
            pl.BlockSpec((nb, tt, D_MODEL), lambda i: (0, i, 0)),
            _const_spec((1, D_MODEL)),
            _const_spec((D_MODEL, SSM_WIDTH)),
            _const_spec((SSM_QUARTERS, Q_IN, 2 * Q_STATE)),
            _const_spec((2, n_state)),
            _const_spec((SSM_QUARTERS, 2 * Q_STATE, Q_IN)),
            _const_spec((1, SSM_WIDTH)),
            _const_spec((SSM_WIDTH, 2 * D_MODEL)),
            _const_spec((nb, n_state)),
            _const_spec((nb, n_state)),
        ],
        out_specs=[
            pl.BlockSpec((nb, tt, D_MODEL), lambda i: (0, i, 0)),
            pl.BlockSpec((nb, n_state), lambda i: (0, 0)),
            pl.BlockSpec((nb, n_state), lambda i: (0, 0)),
        ],
        out_shape=[
            jax.ShapeDtypeStruct((nb, seq, D_MODEL), F32),
            jax.ShapeDtypeStruct((nb, n_state), F32),
            jax.ShapeDtypeStruct((nb, n_state), F32),
        ],
        scratch_shapes=[
            pltpu.VMEM((SSM_WIDTH // LANES, nb * _s5_pitch(tt), LANES), F32),
            pltpu.VMEM((rows, SSM_WIDTH), F32),
            pltpu.VMEM((2, rows, 2 * Q_STATE), F32),
            pltpu.VMEM((rows, SSM_WIDTH), F32),
            pltpu.VMEM((rows, SSM_WIDTH), BF16),
            pltpu.VMEM((nb, n_state), F32),
            pltpu.VMEM((nb, n_state), F32),
        ],
        compiler_params=pltpu.CompilerParams(
            dimension_semantics=("arbitrary",), vmem_limit_bytes=VMEM_LIMIT),
        name="s5_mixer",
    )(h, g.reshape(1, D_MODEL), win, wb, lam, wc, d.reshape(1, SSM_WIDTH), wout, x0r, x0i)


def _s5_discretise(lam_re, lam_im, b_re, b_im, c_re, c_im, log_step):
    lr, li = lam_re.astype(F32), lam_im.astype(F32)
    step = jnp.exp(log_step.astype(F32))[:, None]
    mag = jnp.exp(lr * step)
    ar = mag * jnp.cos(li * step)
    ai = mag * jnp.sin(li * step)
    den = lr * lr + li * li
    nr, ni = ar - 1.0, ai
    cr = (nr * lr + ni * li) / den
    ci = (ni * lr - nr * li) / den
    br, bi = b_re.astype(F32), b_im.astype(F32)
    bbar_r = cr[..., None] * br - ci[..., None] * bi
    bbar_i = cr[..., None] * bi + ci[..., None] * br
    eye = jnp.eye(Q_GROUPS, dtype=F32)

    def expand(bb):
        t = jnp.einsum('qgpc,gh->qgchp', bb.reshape(SSM_QUARTERS, Q_GROUPS, SSM_STATE, SSM_GROUP), eye)
        return t.reshape(SSM_QUARTERS, Q_IN, Q_STATE)

    def contract(cc):
        t = jnp.einsum('qgcp,gh->qgphc', cc.reshape(SSM_QUARTERS, Q_GROUPS, SSM_GROUP, SSM_STATE), eye)
        return t.reshape(SSM_QUARTERS, Q_STATE, Q_IN)

    wb = jnp.concatenate([expand(bbar_r), expand(bbar_i)], axis=-1).astype(BF16)
    wc = jnp.concatenate([contract(c_re.astype(F32)), -contract(c_im.astype(F32))], axis=1).astype(BF16)
    lam = jnp.stack([ar.reshape(-1), ai.reshape(-1)])
    return wb, lam, wc


def _head_norm_rope(x, gain, cos, sin_signed):
    lane = lax.broadcasted_iota(jnp.int32, x.shape, 1)
    ss = x * x
    shift = HEAD_DIM // 2
    while shift:
        partner = jnp.where((lane & shift) == 0, pltpu.roll(ss, LANES - shift, 1), pltpu.roll(ss, shift, 1))
        ss = ss + partner
        shift //= 2
    xn = x * lax.rsqrt(ss * (1.0 / HEAD_DIM) + EPS) * gain
    first_half = (lane % HEAD_DIM) < (HEAD_DIM // 2)
    partner = jnp.where(first_half,
                        pltpu.roll(xn, LANES - HEAD_DIM // 2, 1),
                        pltpu.roll(xn, HEAD_DIM // 2, 1))
    return xn * cos + partner * sin_signed


def _rope_tables(pos):
    half = HEAD_DIM // 2
    freqs = ROPE_THETA ** (-jnp.arange(0, half, dtype=F32) * 2.0 / HEAD_DIM)
    ang = pos.astype(F32)[:, None] * freqs[None, :]
    cos, sin = jnp.cos(ang), jnp.sin(ang)
    cos_t = jnp.tile(cos, (1, 2 * LANES // HEAD_DIM))
    sin_t = jnp.tile(jnp.concatenate([-sin, sin], axis=-1), (1, LANES // HEAD_DIM))
    return cos_t, sin_t


def _kv_compute(hn, w_kv, k_gain, cos, sin_signed):
    kv = jnp.dot(hn, w_kv, preferred_element_type=F32)
    k = jnp.concatenate(
        [_head_norm_rope(kv[:, j * LANES:(j + 1) * LANES], k_gain, cos, sin_signed)
         for j in range(KV_WIDTH // LANES)], axis=1)
    return k.astype(BF16), kv[:, KV_WIDTH:]


def _kv_body(h_ref, g_ref, w_ref, kg_ref, cos_ref, sin_ref, k_ref, v_ref):
    hn = _rms(h_ref[...], g_ref[...]).astype(BF16)
    k, v = _kv_compute(hn, w_ref[...], kg_ref[...], cos_ref[...], sin_ref[...])
    k_ref[...] = k
    v_ref[...] = v.astype(BF16)


def _kv_call(h2d, g, w_kv, kg128, cos_t, sin_t):
    rows = h2d.shape[0]
    return pl.pallas_call(
        _kv_body,
        grid=(1,),
        in_specs=[
            _const_spec((rows, D_MODEL)),
            _const_spec((1, D_MODEL)),
            _const_spec((D_MODEL, 2 * KV_WIDTH)),
            _const_spec((1, LANES)),
            _const_spec((rows, LANES)),
            _const_spec((rows, LANES)),
        ],
        out_specs=[pl.BlockSpec((rows, KV_WIDTH), lambda i: (0, 0)),
                   pl.BlockSpec((rows, KV_WIDTH), lambda i: (0, 0))],
        out_shape=[jax.ShapeDtypeStruct((rows, KV_WIDTH), BF16),
                   jax.ShapeDtypeStruct((rows, KV_WIDTH), BF16)],
        compiler_params=pltpu.CompilerParams(
            dimension_semantics=("arbitrary",), vmem_limit_bytes=VMEM_LIMIT),
        name="shared_kv",
    )(h2d, g.reshape(1, D_MODEL), w_kv, kg128, cos_t, sin_t)


HEAD_PAIRS = N_Q_HEADS // 2
ATTN_SUB = 8
ATTN_UNIT_PAIRS = 2


def _attn_body(h_ref, g_ref, wq_ref, qg_ref, cos_ref, sin_ref, bias_ref, sink_ref,
               kp_ref, kc_ref, km_ref, vp_ref, vc_ref, vm_ref, wo_ref, o_ref, qp_scr, ot_scr):
    n = pl.program_id(1)

    @pl.when((pl.program_id(0) == 0) & (n == 0))
    def _():
        qp_scr[...] = jnp.zeros_like(qp_scr)

    gain = qg_ref[...]
    half = HEAD_DIM // 2
    hn = _rms(h_ref[0], g_ref[...]).astype(BF16)
    q_all = jnp.dot(hn, wq_ref[...], preferred_element_type=F32)

    def scores_of(unit):
        s, part = divmod(unit, HEAD_PAIRS // ATTN_UNIT_PAIRS)
        pairs = range(part * ATTN_UNIT_PAIRS, (part + 1) * ATTN_UNIT_PAIRS)
        rows = slice(s * BLOCK, (s + 1) * BLOCK)
        col0 = pairs[0] * 2 * HEAD_DIM
        qt = q_all[rows, col0:col0 + len(pairs) * 2 * HEAD_DIM].T
        cos, sin = cos_ref[:, rows], sin_ref[:, rows]
        for hd in range(2 * pairs[0], 2 * pairs[-1] + 2):
            x = qt[hd * HEAD_DIM - col0:(hd + 1) * HEAD_DIM - col0, :]
            xn = x * lax.rsqrt(jnp.mean(x * x, axis=0, keepdims=True) + EPS) * gain
            x1, x2 = xn[:half], xn[half:]
            r = jnp.concatenate([x1 * cos - x2 * sin, x2 * cos + x1 * sin], axis=0).astype(BF16)
            row0 = ((hd // Q_PER_KV) % 2) * HEAD_DIM
            qp_scr[s, hd // 2, row0:row0 + HEAD_DIM, (hd % 2) * LANES:(hd % 2 + 1) * LANES] = r
        if s == 0:
            k_prev = lambda l0: kp_ref[0, :, l0:l0 + LANES]
            bias = bias_ref[jnp.minimum(n, 1)]
        else:
            k_prev = lambda l0: kc_ref[0, (s - 1) * BLOCK:s * BLOCK, l0:l0 + LANES]
            bias = bias_ref[1]
        out = []
        for p in pairs:
            hk = p // (Q_PER_KV // 2)
            lane0 = (hk // 2) * LANES
            rhs = qp_scr[s, p]
            s_p = jnp.dot(k_prev(lane0), rhs, preferred_element_type=F32) + bias[:BLOCK]
            s_c = jnp.dot(kc_ref[0, rows, lane0:lane0 + LANES], rhs, preferred_element_type=F32) + bias[BLOCK:]
            s_m = jnp.dot(km_ref[:, lane0:lane0 + LANES], rhs, preferred_element_type=F32)
            out.append((s_p, s_c, s_m))
        return out

    def finish(unit, scores):
        s, part = divmod(unit, HEAD_PAIRS // ATTN_UNIT_PAIRS)
        rows = slice(s * BLOCK, (s + 1) * BLOCK)
        for i, p in enumerate(range(part * ATTN_UNIT_PAIRS, (part + 1) * ATTN_UNIT_PAIRS)):
            hk = p // (Q_PER_KV // 2)
            s_p, s_c, s_m = scores[i]
            sink = sink_ref[p:p + 1, :]
            m = jnp.maximum(
                jnp.maximum(jnp.max(s_p, axis=0, keepdims=True), jnp.max(s_c, axis=0, keepdims=True)),
                jnp.maximum(jnp.max(s_m, axis=0, keepdims=True), sink))
            e_p = jnp.exp2(s_p - m)
            e_c = jnp.exp2(s_c - m)
            e_m = jnp.exp2(s_m - m)
            den = (jnp.sum(e_p, axis=0, keepdims=True) + jnp.sum(e_c, axis=0, keepdims=True)
                   + jnp.sum(e_m, axis=0, keepdims=True) + jnp.exp2(sink - m))
            v0 = hk * HEAD_DIM
            if s == 0:
                v_prev = vp_ref[0, v0:v0 + HEAD_DIM, :]
            else:
                v_prev = vc_ref[0, v0:v0 + HEAD_DIM, (s - 1) * BLOCK:s * BLOCK]
            ot = (jnp.dot(v_prev, e_p.astype(BF16), preferred_element_type=F32)
                  + jnp.dot(vc_ref[0, v0:v0 + HEAD_DIM, rows], e_c.astype(BF16), preferred_element_type=F32)
                  + jnp.dot(vm_ref[v0:v0 + HEAD_DIM, :], e_m.astype(BF16), preferred_element_type=F32))
            ot = ot * (1.0 / den)
            ot_scr[s, 2 * p * HEAD_DIM:(2 * p + 1) * HEAD_DIM, :] = ot[:, :LANES]
            ot_scr[s, (2 * p + 1) * HEAD_DIM:(2 * p + 2) * HEAD_DIM, :] = ot[:, LANES:]
        if part + 1 == HEAD_PAIRS // ATTN_UNIT_PAIRS:
            o = ot_scr[s].T.astype(BF16)
            o_ref[0, rows, :] = h_ref[0, rows, :] + jnp.dot(o, wo_ref[...], preferred_element_type=F32)

    n_units = ATTN_SUB * HEAD_PAIRS // ATTN_UNIT_PAIRS
    queue = [scores_of(0), scores_of(1)]
    for u in range(n_units):
        if u + 2 < n_units:
            queue.append(scores_of(u + 2))
        finish(u, queue.pop(0))


def _attn_tables(q_gain, sinks, pos):
    half = HEAD_DIM // 2
    freqs = ROPE_THETA ** (-jnp.arange(0, half, dtype=F32) * 2.0 / HEAD_DIM)
    ang = pos.astype(F32)[:, None] * freqs[None, :]
    cos_t, sin_t = jnp.cos(ang).T, jnp.sin(ang).T
    log2e = math.log2(math.e)
    gain = jnp.broadcast_to((q_gain.astype(F32) * (HEAD_DIM ** -0.5 * log2e))[:, None], (HEAD_DIM, LANES))
    kj = jnp.arange(BLOCK)[:, None]
    qi = jnp.arange(BLOCK)[None, :]
    neg = jnp.full((BLOCK, BLOCK), NEG_INF, F32)
    prev = jnp.where(kj > qi, 0.0, NEG_INF).astype(F32)
    cur = jnp.where(kj <= qi, 0.0, NEG_INF).astype(F32)
    bias = jnp.stack([jnp.concatenate([neg, cur]), jnp.concatenate([prev, cur])])
    bias = jnp.tile(bias, (1, 1, 2))
    sink_t = jnp.repeat((sinks.astype(F32) * log2e).reshape(HEAD_PAIRS, 2), LANES, axis=1)
    return cos_t, sin_t, gain, bias, sink_t


def _attn_call(h, g, wq, tables, k, vt, k_meta, vt_meta, wo):
    nb, seq, _ = h.shape
    qb = ATTN_SUB * BLOCK
    assert seq % qb == 0
    cos_t, sin_t, gain, bias, sink_t = tables
    prev = lambda n: jnp.maximum(ATTN_SUB * n - 1, 0)
    half = HEAD_DIM // 2
    return pl.pallas_call(
        _attn_body,
        grid=(nb, seq // qb),
        in_specs=[
            pl.BlockSpec((1, qb, D_MODEL), lambda b, n: (b, n, 0)),
            _const_spec((1, D_MODEL)),
            _const_spec((D_MODEL, D_MODEL)),
            _const_spec((HEAD_DIM, LANES)),
            pl.BlockSpec((half, qb), lambda b, n: (0, n)),
            pl.BlockSpec((half, qb), lambda b, n: (0, n)),
            _const_spec((2, 2 * BLOCK, 2 * LANES)),
            _const_spec((HEAD_PAIRS, 2 * LANES)),
            pl.BlockSpec((1, BLOCK, KV_WIDTH), lambda b, n: (b, prev(n), 0)),
            pl.BlockSpec((1, qb, KV_WIDTH), lambda b, n: (b, n, 0)),
            _const_spec((N_META, KV_WIDTH)),
            pl.BlockSpec((1, KV_WIDTH, BLOCK), lambda b, n: (b, 0, prev(n))),
            pl.BlockSpec((1, KV_WIDTH, qb), lambda b, n: (b, 0, n)),
            _const_spec((KV_WIDTH, N_META)),
            _const_spec((D_MODEL, D_MODEL)),
        ],
        out_specs=pl.BlockSpec((1, qb, D_MODEL), lambda b, n: (b, n, 0)),
        out_shape=jax.ShapeDtypeStruct((nb, seq, D_MODEL), F32),
        scratch_shapes=[pltpu.VMEM((ATTN_SUB, HEAD_PAIRS, LANES, 2 * LANES), BF16),
                        pltpu.VMEM((ATTN_SUB, D_MODEL, BLOCK), F32)],
        compiler_params=pltpu.CompilerParams(
            dimension_semantics=("arbitrary", "arbitrary"), vmem_limit_bytes=VMEM_LIMIT),
        name="swa_attention",
    )(h, g.reshape(1, D_MODEL), wq, gain, cos_t, sin_t, bias, sink_t, k, k, k_meta, vt, vt, vt_meta, wo)


def kernel(x, meta_tokens, ffn1_norm, ffn1_w_gate_up, ffn1_w_down, mix_norm, ffn2_norm, ffn2_w_gate_up,
           ffn2_w_down, ssm_w_in, ssm_lambda_re, ssm_lambda_im, ssm_b_re, ssm_b_im, ssm_c_re, ssm_c_im,
           ssm_log_step, ssm_d, ssm_w_out, kv_norm, w_kv, k_norm, attn_w_q, q_norm, attn_sinks, attn_w_o):
    nb, seq, _ = x.shape
    rows = nb * seq
    tm = 1024
    tt = 64
    assert seq % tm == 0 and seq % tt == 0
    bf = lambda w: w.astype(BF16)
    n_state = SSM_GROUPS * SSM_STATE

    wb, lam, wc = _s5_discretise(ssm_lambda_re[0], ssm_lambda_im[0], ssm_b_re[0], ssm_b_im[0],
                                 ssm_c_re[0], ssm_c_im[0], ssm_log_step[0])
    s5_w = (mix_norm[0], bf(ssm_w_in[0]))
    s5_w2 = (wb, lam, wc, ssm_d[0].astype(F32), bf(ssm_w_out[0]))
    ffn_w = ((bf(ffn1_w_gate_up), bf(ffn1_w_down)), (bf(ffn2_w_gate_up), bf(ffn2_w_down)))
    ffn = lambda h2d, which, layer, t: _ffn_call(
        h2d, (ffn1_norm, ffn2_norm)[which][layer], *ffn_w[which], layer, t)
    w_kv_b = bf(w_kv)
    kg128 = jnp.tile(k_norm.astype(F32), LANES // HEAD_DIM).reshape(1, LANES)

    hm = ffn(meta_tokens.astype(F32), 0, 0, N_META)
    hm_b = jnp.broadcast_to(hm[None], (nb, N_META, D_MODEL))
    zeros = jnp.zeros((nb, n_state), F32)
    hm_b, x0r, x0i = _s5_call(hm_b, *s5_w, *s5_w2, zeros, zeros, N_META)
    hm = ffn(hm_b[0], 1, 0, N_META)
    cos_m, sin_m = _rope_tables(jnp.arange(N_META))
    k_meta, v_meta = _kv_call(hm, kv_norm, w_kv_b, kg128, cos_m, sin_m)

    h = ffn(x.reshape(rows, D_MODEL), 0, 0, tm)
    h, _, _ = _s5_call(h.reshape(nb, seq, D_MODEL), *s5_w, *s5_w2, x0r, x0i, tt)
    h = ffn(h.reshape(rows, D_MODEL), 1, 0, tm)

    cos_t, sin_t = _rope_tables(N_META + jnp.arange(seq))
    h, k, vt = _ffn_call(h, ffn1_norm[1], *ffn_w[0], 1, tm,
                         kv=(kv_norm, w_kv_b, kg128, cos_t, sin_t, seq))
    k = k.reshape(nb, seq, KV_WIDTH)
    tables = _attn_tables(q_norm[0], attn_sinks[0], N_META + jnp.arange(seq))
    h = _attn_call(h.reshape(nb, seq, D_MODEL), mix_norm[1], bf(attn_w_q[0]), tables,
                   k, vt, k_meta, v_meta.T, bf(attn_w_o[0]))
    h = ffn(h.reshape(rows, D_MODEL), 1, 1, tm)
    return h.reshape(nb, seq, D_MODEL)
```
